```python
import math
import jax, jax.numpy as jnp
from jax import lax
import numpy as np

D_MODEL = 2048
BATCH = 4
SEQ = 4096
DEPTH = 4

N_AB_LAYERS = (DEPTH + 1) // 2
N_C_LAYERS = DEPTH // 2
MIX_WIDTH = D_MODEL
ATT_WIDTH = MIX_WIDTH // 2
SSM_WIDTH = MIX_WIDTH - ATT_WIDTH
IN_WIDTH = 3 * ATT_WIDTH + SSM_WIDTH
N_DIFF_HEADS = 8
DIFF_HEAD_DIM = ATT_WIDTH // (2 * N_DIFF_HEADS)
Q_BLOCK = 128
N_BUCKETS = 32
MAX_DISTANCE = 128
SSM_GROUP = 16
N_SSM_GROUPS = SSM_WIDTH // SSM_GROUP
SSM_STATE = 64
CONV_KERNEL = 31
FFN_HIDDEN = 5632
FFN_CONV = 3
EPS = 1e-6

kernel_name = 'hybrid_diffattn_s5_conformer_convffn'


def rmsnorm(x, g):
    xf = x.astype(jnp.float32)
    r = xf * lax.rsqrt(jnp.mean(xf * xf, axis=-1, keepdims=True) + EPS)
    return (r * g.astype(jnp.float32)).astype(x.dtype)


def layernorm(x, g, b):
    xf = x.astype(jnp.float32)
    mu = jnp.mean(xf, axis=-1, keepdims=True)
    var = jnp.mean(jnp.square(xf - mu), axis=-1, keepdims=True)
    r = (xf - mu) * lax.rsqrt(var + EPS)
    return (r * g.astype(jnp.float32) + b.astype(jnp.float32)).astype(x.dtype)


def causal_dwconv(x, w, b):
    k_width, ch = w.shape
    y = lax.conv_general_dilated(
        x, w[:, None, :].astype(x.dtype), window_strides=(1,),
        padding=[(k_width - 1, 0)], dimension_numbers=('NWC', 'WIO', 'NWC'),
        feature_group_count=ch)
    return y + b.astype(x.dtype)


def t5_bucket(rel):
    n = jnp.maximum(rel, 0)
    max_exact = N_BUCKETS // 2
    nf = jnp.maximum(n, 1).astype(jnp.float32)
    large = max_exact + (jnp.log(nf / max_exact) / math.log(MAX_DISTANCE / max_exact)
                         * (N_BUCKETS - max_exact)).astype(jnp.int32)
    large = jnp.minimum(large, N_BUCKETS - 1)
    return jnp.where(n < max_exact, n, large)


def diff_attention(q, k, v, rel_bias, lam):
    bsz, slen, nh, _, hd = q.shape
    nb = slen // Q_BLOCK
    scale = hd ** -0.5
    k1 = k[:, :, :, 0]
    k2 = k[:, :, :, 1]
    vf = v.astype(jnp.float32)
    qb = q.reshape(bsz, nb, Q_BLOCK, nh, 2, hd).transpose(1, 0, 2, 3, 4, 5)
    k_pos = jnp.arange(slen)

    def block(args):
        q_blk, blk = args
        q_pos = blk * Q_BLOCK + jnp.arange(Q_BLOCK)
        rel = q_pos[:, None] - k_pos[None, :]
        bias = rel_bias[t5_bucket(rel)].astype(jnp.float32).transpose(2, 0, 1)
        causal = rel >= 0

        def probs(qh, kh):
            logits = jnp.einsum('bqhd,bkhd->bhqk', qh, kh).astype(jnp.float32) * scale + bias
            logits = jnp.where(causal, logits, -jnp.inf)
            return jax.nn.softmax(logits, axis=-1)

        p = probs(q_blk[:, :, :, 0], k1) - lam * probs(q_blk[:, :, :, 1], k2)
        return jnp.einsum('bhqk,bkhv->bqhv', p, vf)

    out = lax.map(block, (qb, jnp.arange(nb)))
    return out.transpose(1, 0, 2, 3, 4).reshape(bsz, slen, nh, 2 * hd)


def s5_mixer(u, lre, lim, log_step, bre, bim, cre, cim, dd, w_glu, b_glu):
    bsz, slen, _ = u.shape
    uf = u.astype(jnp.float32).reshape(bsz, slen, N_SSM_GROUPS, SSM_GROUP)
    lam = lax.complex(lre.astype(jnp.float32), lim.astype(jnp.float32))
    dt = jnp.exp(log_step.astype(jnp.float32))[:, None]
    lam_bar = jnp.exp(lam * dt)
    b_mat = lax.complex(bre.astype(jnp.float32), bim.astype(jnp.float32))
    b_bar = ((lam_bar - 1.0) / lam)[..., None] * b_mat
    bu = jnp.einsum('gph,bsgh->bsgp', b_bar, uf.astype(jnp.complex64))
    a = jnp.broadcast_to(lam_bar, (1, slen) + lam_bar.shape)

    def combine(e1, e2):
        a1, b1 = e1
        a2, b2 = e2
        return a1 * a2, a2 * b1 + b2

    _, states = lax.associative_scan(combine, (a, bu), axis=1)
    c_mat = lax.complex(cre.astype(jnp.float32), cim.astype(jnp.float32))
    y = jnp.einsum('ghp,bsgp->bsgh', c_mat, states).real + dd.astype(jnp.float32) * uf
    y = y.reshape(bsz, slen, SSM_WIDTH)
    g = jax.nn.gelu(y)
    out = g * jax.nn.sigmoid(g @ w_glu.astype(jnp.float32) + b_glu.astype(jnp.float32))
    return out.astype(u.dtype)


def conformer_conv(h, w_pw1, w_dw, b_dw, ln_g, ln_b, w_pw2):
    a, gate = jnp.split(h @ w_pw1, 2, axis=-1)
    z = a * jax.nn.sigmoid(gate)
    z = causal_dwconv(z, w_dw, b_dw)
    z = layernorm(z, ln_g, ln_b)
    return jax.nn.silu(z) @ w_pw2


def conv_ffn(h, w_up, w_dw, b_dw, w_down):
    up = causal_dwconv(h @ w_up, w_dw, b_dw)
    gate, val = jnp.split(up, 2, axis=-1)
    return (jax.nn.silu(gate) * val) @ w_down


def setup_inputs(seed: int = 0) -> dict:
    key = jax.random.key(seed)
    ks = iter(jax.random.split(key, 48))
    f32 = jnp.float32

    def nrm(shape, scale):
        return jax.random.normal(next(ks), shape, f32) * scale

    def gain(shape):
        return 1.0 + nrm(shape, 0.02)

    na, nc = N_AB_LAYERS, N_C_LAYERS
    G, P, H16 = N_SSM_GROUPS, SSM_STATE, SSM_GROUP
    lam_im0 = jnp.pi * jnp.arange(P, dtype=f32)
    return {
        'x': nrm((BATCH, SEQ, D_MODEL), 1.0),
        'rel_bias': nrm((N_BUCKETS, N_DIFF_HEADS), 0.5),
        'norm_mix': gain((DEPTH, D_MODEL)),
        'norm_ffn': gain((DEPTH, D_MODEL)),
        'norm_final': gain((D_MODEL,)),
        'ab_w_in': nrm((na, D_MODEL, IN_WIDTH), D_MODEL ** -0.5),
        'ab_w_out': nrm((na, MIX_WIDTH, D_MODEL), MIX_WIDTH ** -0.5),
        'diff_lq1': nrm((na, DIFF_HEAD_DIM), 0.1),
        'diff_lk1': nrm((na, DIFF_HEAD_DIM), 0.1),
        'diff_lq2': nrm((na, DIFF_HEAD_DIM), 0.1),
        'diff_lk2': nrm((na, DIFF_HEAD_DIM), 0.1),
        'diff_head_norm': gain((na, 2 * DIFF_HEAD_DIM)),
        's5_lambda_re': -0.5 + nrm((na, G, P), 0.01),
        's5_lambda_im': lam_im0 + nrm((na, G, P), 0.01),
        's5_log_step': jax.random.uniform(next(ks), (na, G), f32, math.log(0.001), math.log(0.1)),
        's5_b_re': nrm((na, G, P, H16), (2 * H16) ** -0.5),
        's5_b_im': nrm((na, G, P, H16), (2 * H16) ** -0.5),
        's5_c_re': nrm((na, G, H16, P), (2 * P) ** -0.5),
        's5_c_im': nrm((na, G, H16, P), (2 * P) ** -0.5),
        's5_d': nrm((na, G, H16), 0.5),
        's5_w_glu': nrm((na, SSM_WIDTH, SSM_WIDTH), SSM_WIDTH ** -0.5),
        's5_b_glu': nrm((na, SSM_WIDTH), 0.02),
        'conv_w_pw1': nrm((nc, D_MODEL, 2 * D_MODEL), D_MODEL ** -0.5),
        'conv_w_dw': nrm((nc, CONV_KERNEL, D_MODEL), CONV_KERNEL ** -0.5),
        'conv_b_dw': nrm((nc, D_MODEL), 0.02),
        'conv_ln_g': gain((nc, D_MODEL)),
        'conv_ln_b': nrm((nc, D_MODEL), 0.02),
        'conv_w_pw2': nrm((nc, D_MODEL, D_MODEL), D_MODEL ** -0.5),
        'ffn_w_up': nrm((DEPTH, D_MODEL, 2 * FFN_HIDDEN), D_MODEL ** -0.5),
        'ffn_w_dw': nrm((DEPTH, FFN_CONV, 2 * FFN_HIDDEN), FFN_CONV ** -0.5),
        'ffn_b_dw': nrm((DEPTH, 2 * FFN_HIDDEN), 0.02),
        'ffn_w_down': nrm((DEPTH, FFN_HIDDEN, D_MODEL), FFN_HIDDEN ** -0.5),
    }


def reference(x, rel_bias, norm_mix, norm_ffn, norm_final, ab_w_in, ab_w_out,
              diff_lq1, diff_lk1, diff_lq2, diff_lk2, diff_head_norm,
              s5_lambda_re, s5_lambda_im, s5_log_step, s5_b_re, s5_b_im, s5_c_re, s5_c_im,
              s5_d, s5_w_glu, s5_b_glu,
              conv_w_pw1, conv_w_dw, conv_b_dw, conv_ln_g, conv_ln_b, conv_w_pw2,
              ffn_w_up, ffn_w_dw, ffn_b_dw, ffn_w_down):
    bsz, slen, _ = x.shape
    for layer in range(DEPTH):
        i = layer // 2
        h = rmsnorm(x, norm_mix[layer])
        if layer % 2 == 0:
            proj = h @ ab_w_in[i]
            q, k, v, u = jnp.split(proj, [ATT_WIDTH, 2 * ATT_WIDTH, 3 * ATT_WIDTH], axis=-1)
            q = q.reshape(bsz, slen, N_DIFF_HEADS, 2, DIFF_HEAD_DIM)
            k = k.reshape(bsz, slen, N_DIFF_HEADS, 2, DIFF_HEAD_DIM)
            v = v.reshape(bsz, slen, N_DIFF_HEADS, 2 * DIFF_HEAD_DIM)
            lam_init = 0.8 - 0.6 * math.exp(-0.3 * layer)
            lam = (jnp.exp(jnp.sum(diff_lq1[i].astype(jnp.float32) * diff_lk1[i].astype(jnp.float32)))
                   - jnp.exp(jnp.sum(diff_lq2[i].astype(jnp.float32) * diff_lk2[i].astype(jnp.float32)))
                   + lam_init)
            att = diff_attention(q, k, v, rel_bias, lam)
            att = rmsnorm(att, diff_head_norm[i]) * (1.0 - lam_init)
            att = att.reshape(bsz, slen, ATT_WIDTH).astype(x.dtype)
            ssm = s5_mixer(u, s5_lambda_re[i], s5_lambda_im[i], s5_log_step[i],
                           s5_b_re[i], s5_b_im[i], s5_c_re[i], s5_c_im[i], s5_d[i],
                           s5_w_glu[i], s5_b_glu[i])
            mix = jnp.concatenate([att, ssm], axis=-1) @ ab_w_out[i]
        else:
            mix = conformer_conv(h, conv_w_pw1[i], conv_w_dw[i], conv_b_dw[i],
                                 conv_ln_g[i], conv_ln_b[i], conv_w_pw2[i])
        x = x + mix
        h = rmsnorm(x, norm_ffn[layer])
        x = x + conv_ffn(h, ffn_w_up[layer], ffn_w_dw[layer], ffn_b_dw[layer], ffn_w_down[layer])
    return rmsnorm(x, norm_final)
```

```python
import functools
import math

import numpy as np
import jax
import jax.numpy as jnp
from jax import lax
from jax.experimental import pallas as pl
from jax.experimental.pallas import tpu as pltpu

F32 = jnp.float32
BF16 = jnp.bfloat16

N_HEADS = 8
HEAD_DIM = 64
HEAD_WIDTH = 2 * HEAD_DIM
N_BUCKETS = 32
MAX_DISTANCE = 128
SSM_GROUP = 16
SSM_STATE = 64
SSM_CHUNK = 16
EPS = 1e-6

V7X_VMEM_LIMIT_BYTES = 56 * 1024 * 1024
MASK_VALUE = -1e30

_NT_DIMS = (((1,), (1,)), ((), ()))


def _cparams(n_axes):
    return pltpu.CompilerParams(
        dimension_semantics=("arbitrary",) * n_axes,
        vmem_limit_bytes=V7X_VMEM_LIMIT_BYTES)


def _dot(a, b):
    return jnp.dot(a, b, preferred_element_type=F32)


def _norm_mm_kernel(x_ref, g_ref, *rest, glu):
    if glu:
        wa_ref, wb_ref, o_ref, h_ref = rest
    else:
        w_ref, o_ref, h_ref = rest

    @pl.when(pl.program_id(1) == 0)
    def _():
        x = x_ref[...]
        ms = jnp.mean(x * x, axis=-1, keepdims=True)
        h_ref[...] = (x * lax.rsqrt(ms + EPS) * g_ref[...]).astype(h_ref.dtype)

    h = h_ref[...]
    if glu:
        a = _dot(h, wa_ref[...])
        gate = _dot(h, wb_ref[...])
        o_ref[...] = (a * jax.nn.sigmoid(gate)).astype(o_ref.dtype)
    else:
        o_ref[...] = _dot(h, w_ref[...]).astype(o_ref.dtype)


def _norm_matmul(x, g, w, *, glu=False, tm, tn, name):
    m, k = x.shape
    n = w.shape[1] // 2 if glu else w.shape[1]
    in_specs = [pl.BlockSpec((tm, k), lambda i, j: (i, 0)),
                pl.BlockSpec((1, k), lambda i, j: (0, 0)),
                pl.BlockSpec((k, tn), lambda i, j: (0, j))]
    args = [x, g.reshape(1, k), w]
    if glu:
        in_specs.append(pl.BlockSpec((k, tn), lambda i, j: (0, j + n // tn)))
        args.append(w)
    return pl.pallas_call(
        functools.partial(_norm_mm_kernel, glu=glu),
        grid=(m // tm, n // tn),
        in_specs=in_specs,
        out_specs=pl.BlockSpec((tm, tn), lambda i, j: (i, j)),
        out_shape=jax.ShapeDtypeStruct((m, n), BF16),
        scratch_shapes=[pltpu.VMEM((tm, k), BF16)],
        compiler_params=_cparams(2),
        name=name,
    )(*args)


def _mm_res_kernel(*refs, n_pairs):
    res_ref, o_ref = refs[2 * n_pairs], refs[2 * n_pairs + 1]
    acc = res_ref[...]
    for p in range(n_pairs):
        acc = acc + _dot(refs[2 * p][...], refs[2 * p + 1][...])
    o_ref[...] = acc


def _matmul_residual(pairs, res, *, tm, tn, name):
    m, n = res.shape
    in_specs, args = [], []
    for a, w in pairs:
        k = a.shape[1]
        in_specs += [pl.BlockSpec((tm, k), lambda i, j: (i, 0)),
                     pl.BlockSpec((k, tn), lambda i, j: (0, j))]
        args += [a, w]
    in_specs.append(pl.BlockSpec((tm, tn), lambda i, j: (i, j)))
    args.append(res)
    return pl.pallas_call(
        functools.partial(_mm_res_kernel, n_pairs=len(pairs)),
        grid=(m // tm, n // tn),
        in_specs=in_specs,
        out_specs=pl.BlockSpec((tm, tn), lambda i, j: (i, j)),
        out_shape=jax.ShapeDtypeStruct((m, n), F32),
        compiler_params=_cparams(2),
        name=name,
    )(*args)


def _glu_mm_kernel(g_ref, w_ref, b_ref, o_ref):
    g = g_ref[...]
    z = _dot(g, w_ref[...]) + b_ref[...]
    o_ref[...] = (g.astype(F32) * jax.nn.sigmoid(z)).astype(o_ref.dtype)


def _glu_matmul(g, w, b, *, tm, name):
    m, k = g.shape
    n = w.shape[1]
    return pl.pallas_call(
        _glu_mm_kernel,
        grid=(m // tm,),
        in_specs=[pl.BlockSpec((tm, k), lambda i: (i, 0)),
                  pl.BlockSpec((k, n), lambda i: (0, 0)),
                  pl.BlockSpec((1, n), lambda i: (0, 0))],
        out_specs=pl.BlockSpec((tm, n), lambda i: (i, 0)),
        out_shape=jax.ShapeDtypeStruct((m, n), BF16),
        compiler_params=_cparams(1),
        name=name,
    )(g, w, b.reshape(1, n))


def _t5_bucket_np(rel):
    n = np.maximum(rel, 0)
    max_exact = N_BUCKETS // 2
    nf = np.maximum(n, 1).astype(np.float32)
    large = max_exact + (np.log(nf / np.float32(max_exact))
                         / np.float32(math.log(MAX_DISTANCE / max_exact))
                         * np.float32(N_BUCKETS - max_exact)).astype(np.int32)
    large = np.minimum(large, N_BUCKETS - 1)
    return np.where(n < max_exact, n, large)


def _bias_tables(rel_bias, slen, t):
    i = np.arange(t)[:, None]
    j = np.arange(t)[None, :]
    far = _t5_bucket_np(np.arange(t + 1, max(slen, t + 2)))
    far_bucket = int(far[0])
    assert np.all(far == far_bucket)
    rb = rel_bias.astype(F32)
    d0 = jnp.where(jnp.asarray(i >= j)[..., None], rb[_t5_bucket_np(i - j)], MASK_VALUE)
    d1 = rb[_t5_bucket_np(i - j + t)]
    tiles = jnp.stack([d0, d1], axis=0).transpose(3, 0, 1, 2)
    return tiles, rb[far_bucket]


def _attn_kernel(cfar_ref, q_ref, k_ref, v_ref, d_ref, lq1_ref, lk1_ref, lq2_ref, lk2_ref,
                 hn_ref, o_ref, acc_ref, m_ref, l_ref, *, t, lam_init):
    head = pl.program_id(1)
    qi = pl.program_id(2)

    q = q_ref[0] * jnp.asarray(HEAD_DIM ** -0.5, q_ref.dtype)
    lane = lax.broadcasted_iota(jnp.int32, q.shape, 1)
    zero = jnp.zeros_like(q)
    q_halves = (jnp.where(lane < HEAD_DIM, q, zero), jnp.where(lane >= HEAD_DIM, q, zero))

    acc_ref[...] = jnp.zeros_like(acc_ref)
    l_ref[...] = jnp.zeros_like(l_ref)
    m_ref[...] = jnp.full_like(m_ref, MASK_VALUE)

    def block(start, bias):
        kb = k_ref[0, pl.ds(start, t), :]
        vb = v_ref[0, pl.ds(start, t), :]
        for idx in range(2):
            s = lax.dot_general(q_halves[idx], kb, _NT_DIMS, preferred_element_type=F32) + bias
            m_old = m_ref[idx]
            m_new = jnp.maximum(m_old, jnp.max(s, axis=-1, keepdims=True))
            alpha = jnp.exp(m_old - m_new)
            p = jnp.exp(s - m_new)
            l_ref[idx] = alpha * l_ref[idx] + jnp.sum(p, axis=-1, keepdims=True)
            acc_ref[idx] = alpha * acc_ref[idx] + _dot(p.astype(vb.dtype), vb)
            m_ref[idx] = m_new

    cfar = cfar_ref[head]

    def far_body(ki, carry):
        block(pl.multiple_of(ki * t, t), cfar)
        return carry

    lax.fori_loop(0, qi - 1, far_body, 0)

    @pl.when(qi >= 1)
    def _():
        block(pl.multiple_of((qi - 1) * t, t), d_ref[0, 1])

    block(pl.multiple_of(qi * t, t), d_ref[0, 0])

    lam = (jnp.exp(jnp.sum(lq1_ref[...] * lk1_ref[...], keepdims=True))
           - jnp.exp(jnp.sum(lq2_ref[...] * lk2_ref[...], keepdims=True)) + lam_init)
    o = acc_ref[0] * (1.0 / l_ref[0]) - lam * (acc_ref[1] * (1.0 / l_ref[1]))
    ms = jnp.mean(o * o, axis=-1, keepdims=True)
    o = o * lax.rsqrt(ms + EPS) * hn_ref[...] * (1.0 - lam_init)
    o_ref[0] = o.astype(o_ref.dtype)


def _diff_attention(proj3, rel_bias, lq1, lk1, lq2, lk2, head_norm, *, lam_init, t, name):
    bsz, slen, _ = proj3.shape
    tiles, cfar = _bias_tables(rel_bias, slen, t)
    vec = lambda a: a.astype(F32).reshape(1, -1)
    small = lambda n: pl.BlockSpec((1, n), lambda b, h, qi: (0, 0))
    return pl.pallas_call(
        functools.partial(_attn_kernel, t=t, lam_init=lam_init),
        grid=(bsz, N_HEADS, slen // t),
        in_specs=[pl.BlockSpec(memory_space=pltpu.SMEM),
                  pl.BlockSpec((1, t, HEAD_WIDTH), lambda b, h, qi: (b, qi, h)),
                  pl.BlockSpec((1, slen, HEAD_WIDTH), lambda b, h, qi: (b, 0, N_HEADS + h)),
                  pl.BlockSpec((1, slen, HEAD_WIDTH), lambda b, h, qi: (b, 0, 2 * N_HEADS + h)),
                  pl.BlockSpec((1, 2, t, t), lambda b, h, qi: (h, 0, 0, 0)),
                  small(HEAD_DIM), small(HEAD_DIM), small(HEAD_DIM), small(HEAD_DIM),
                  small(HEAD_WIDTH)],
        out_specs=pl.BlockSpec((1, t, HEAD_WIDTH), lambda b, h, qi: (b, qi, h)),
        out_shape=jax.ShapeDtypeStruct((bsz, slen, N_HEADS * HEAD_WIDTH), BF16),
        scratch_shapes=[pltpu.VMEM((2, t, HEAD_WIDTH), F32),
                        pltpu.VMEM((2, t, 1), F32),
                        pltpu.VMEM((2, t, 1), F32)],
        compiler_params=_cparams(3),
        name=name,
    )(cfar, proj3, proj3, proj3, tiles, vec(lq1), vec(lk1), vec(lq2), vec(lk2), vec(head_norm))


def _s5_operators(lre, lim, log_step, bre, bim, cre, cim, dd, n_levels):
    f = lambda a: a.astype(F32)
    lre, lim, bre, bim, cre, cim = map(f, (lre, lim, bre, bim, cre, cim))
    n_groups = lre.shape[0]
    cl = SSM_CHUNK
    hi = lax.Precision.HIGHEST
    dt = jnp.exp(f(log_step))[:, None]
    zr, zi = lre * dt, lim * dt
    steps = jnp.arange(cl + 1, dtype=F32)[None, :, None]
    mag = jnp.exp(zr[:, None, :] * steps)
    ang = zi[:, None, :] * steps
    pr, pi = mag * jnp.cos(ang), mag * jnp.sin(ang)
    nr = jnp.expm1(zr) * jnp.cos(zi) - 2.0 * jnp.sin(0.5 * zi) ** 2
    ni = jnp.exp(zr) * jnp.sin(zi)
    den = lre * lre + lim * lim
    fr, fi = (nr * lre + ni * lim) / den, (ni * lre - nr * lim) / den
    bbr = fr[..., None] * bre - fi[..., None] * bim
    bbi = fr[..., None] * bim + fi[..., None] * bre
    cpr = cre[:, None] * pr[:, :, None, :] - cim[:, None] * pi[:, :, None, :]
    cpi = cre[:, None] * pi[:, :, None, :] + cim[:, None] * pr[:, :, None, :]
    kern = (jnp.einsum('gjhp,gpi->gjhi', cpr[:, :cl], bbr, precision=hi)
            - jnp.einsum('gjhp,gpi->gjhi', cpi[:, :cl], bbi, precision=hi))
    s_idx = np.arange(cl)[:, None]
    t_idx = np.arange(cl)[None, :]
    lag = np.clip(t_idx - s_idx, 0, cl - 1)
    toep = jnp.where(jnp.asarray(t_idx >= s_idx)[None, :, :, None, None], kern[:, lag], 0.0)
    toep = toep.transpose(0, 1, 4, 2, 3).reshape(n_groups, cl * SSM_GROUP, cl * SSM_GROUP)
    rev = np.arange(cl - 1, -1, -1)
    wr = pr[:, rev][:, :, None, :] * bbr.transpose(0, 2, 1)[:, None] \
        - pi[:, rev][:, :, None, :] * bbi.transpose(0, 2, 1)[:, None]
    wi = pr[:, rev][:, :, None, :] * bbi.transpose(0, 2, 1)[:, None] \
        + pi[:, rev][:, :, None, :] * bbr.transpose(0, 2, 1)[:, None]
    w_in = jnp.concatenate([wr, wi], axis=-1).reshape(n_groups, cl * SSM_GROUP, 2 * SSM_STATE)
    v_re = cpr[:, 1:].transpose(0, 3, 1, 2).reshape(n_groups, SSM_STATE, cl * SSM_GROUP)
    v_im = -cpi[:, 1:].transpose(0, 3, 1, 2).reshape(n_groups, SSM_STATE, cl * SSM_GROUP)
    v_out = jnp.concatenate([v_re, v_im], axis=1)
    ar, ai = pr[:, cl], pi[:, cl]
    ars, ais = [], []
    for _ in range(n_levels):
        ars.append(jnp.concatenate([ar, ar], axis=-1))
        ais.append(jnp.concatenate([-ai, ai], axis=-1))
        ar, ai = ar * ar - ai * ai, 2.0 * ar * ai
    d_row = jnp.tile(f(dd), (1, cl))[:, None, :]
    return (toep.astype(BF16), w_in.astype(BF16), v_out.astype(BF16),
            jnp.stack(ars, axis=1), jnp.stack(ais, axis=1), d_row)


def _s5_kernel(u_ref, t_ref, w_ref, v_ref, ar_ref, ai_ref, d_ref, o_ref, *, n_levels, period):
    u = u_ref[0]
    x = _dot(u, w_ref[0])
    row = lax.rem(lax.broadcasted_iota(jnp.int32, x.shape, 0), period)
    for k in range(n_levels):
        d = 1 << k
        xs = jnp.where(row >= d, pltpu.roll(x, d, axis=0), 0.0)
        x = x + ar_ref[0, k:k + 1, :] * xs + ai_ref[0, k:k + 1, :] * pltpu.roll(xs, SSM_STATE, axis=1)
    xp = jnp.where(row >= 1, pltpu.roll(x, 1, axis=0), 0.0)
    xh = xp.astype(BF16)
    xl = (xp - xh.astype(F32)).astype(BF16)
    v = v_ref[0]
    y = _dot(u, t_ref[0]) + _dot(xh, v) + _dot(xl, v) + d_ref[0] * u.astype(F32)
    o_ref[0] = jax.nn.gelu(y).astype(o_ref.dtype)


def _s5_scan(u, ops, *, bsz, slen, name):
    toep, w_in, v_out, ars, ais, d_row = ops
    n_groups = toep.shape[0]
    cl = SSM_CHUNK
    period = slen // cl
    rows = bsz * period
    n_levels = ars.shape[1]
    width = cl * SSM_GROUP
    uc = u.reshape(rows, cl, n_groups, SSM_GROUP).transpose(2, 0, 1, 3).reshape(n_groups, rows, width)
    per_group = lambda *shape: pl.BlockSpec((1,) + shape, lambda g: (g, 0, 0))
    y = pl.pallas_call(
        functools.partial(_s5_kernel, n_levels=n_levels, period=period),
        grid=(n_groups,),
        in_specs=[per_group(rows, width), per_group(width, width),
                  per_group(width, 2 * SSM_STATE), per_group(2 * SSM_STATE, width),
                  per_group(n_levels, 2 * SSM_STATE), per_group(n_levels, 2 * SSM_STATE),
                  per_group(1, width)],
        out_specs=per_group(rows, width),
        out_shape=jax.ShapeDtypeStruct((n_groups, rows, width), BF16),
        compiler_params=_cparams(1),
        name=name,
    )(uc, toep, w_in, v_out, ars, ais, d_row)
    return y.reshape(n_groups, rows, cl, SSM_GROUP).transpose(1, 2, 0, 3).reshape(bsz * slen, -1)


_CONV_HALO = 32
_CONV_ROWS = 32
_CONV_COLS = 512


def _conv_ln_kernel(prev_ref, cur_ref, w_ref, b_ref, lg_ref, lb_ref, o_ref, ext_ref, y_ref, *, ts):
    kw = w_ref.shape[0]
    n_ch = cur_ref.shape[-1]
    first = pl.program_id(1) == 0
    prev = prev_ref[0].astype(F32)
    ext_ref[0:_CONV_HALO, :] = jnp.where(first, jnp.zeros_like(prev), prev)
    ext_ref[_CONV_HALO:, :] = cur_ref[0].astype(F32)
    base = _CONV_HALO - (kw - 1)
    for c0 in range(0, n_ch, _CONV_COLS):
        cols = slice(c0, c0 + _CONV_COLS)
        for r0 in range(0, ts, _CONV_ROWS):
            acc = jnp.broadcast_to(b_ref[:, cols], (_CONV_ROWS, _CONV_COLS))
            for k in range(kw):
                acc = acc + w_ref[k:k + 1, cols] * ext_ref[r0 + base + k:r0 + base + k + _CONV_ROWS, cols]
            y_ref[r0:r0 + _CONV_ROWS, cols] = acc
    y = y_ref[...]
    mu = jnp.mean(y, axis=-1, keepdims=True)
    yc = y - mu
    var = jnp.mean(yc * yc, axis=-1, keepdims=True)
    r = yc * lax.rsqrt(var + EPS) * lg_ref[...] + lb_ref[...]
    o_ref[0] = (r * jax.nn.sigmoid(r)).astype(o_ref.dtype)


def _conv_ln_silu(z3, w_dw, b_dw, ln_g, ln_b, *, ts, name):
    bsz, slen, n_ch = z3.shape
    kw = w_dw.shape[0]
    assert kw - 1 <= _CONV_HALO
    row = lambda a: a.astype(F32).reshape(1, n_ch)
    full = lambda r: pl.BlockSpec((r, n_ch), lambda b, i: (0, 0))
    per_halo = ts // _CONV_HALO
    return pl.pallas_call(
        functools.partial(_conv_ln_kernel, ts=ts),
        grid=(bsz, slen // ts),
        in_specs=[pl.BlockSpec((1, _CONV_HALO, n_ch),
                               lambda b, i: (b, jnp.maximum(i * per_halo - 1, 0), 0)),
                  pl.BlockSpec((1, ts, n_ch), lambda b, i: (b, i, 0)),
                  full(kw), full(1), full(1), full(1)],
        out_specs=pl.BlockSpec((1, ts, n_ch), lambda b, i: (b, i, 0)),
        out_shape=jax.ShapeDtypeStruct((bsz, slen, n_ch), BF16),
        scratch_shapes=[pltpu.VMEM((ts + _CONV_HALO, n_ch), F32),
                        pltpu.VMEM((ts, n_ch), F32)],
        compiler_params=_cparams(2),
        name=name,
    )(z3, z3, w_dw.astype(F32), row(b_dw), row(ln_g), row(ln_b))


_FFN_HALO = 16


def _conv3_gate_kernel(pg_ref, pv_ref, g_ref, v_ref, wg_ref, wv_ref, bg_ref, bv_ref, o_ref):
    first = pl.program_id(1) == 0

    def conv(cur_ref, prev_ref, w_ref, b_ref):
        cur = cur_ref[0].astype(F32)
        prev = prev_ref[0].astype(F32)
        prev = jnp.where(first, jnp.zeros_like(prev), prev)
        p1 = prev[_FFN_HALO - 1:_FFN_HALO]
        p2 = prev[_FFN_HALO - 2:_FFN_HALO - 1]
        row = lax.broadcasted_iota(jnp.int32, cur.shape, 0)
        x1 = jnp.where(row == 0, p1, pltpu.roll(cur, 1, axis=0))
        x2 = jnp.where(row == 0, p2, jnp.where(row == 1, p1, pltpu.roll(cur, 2, axis=0)))
        return w_ref[0:1] * x2 + w_ref[1:2] * x1 + w_ref[2:3] * cur + b_ref[...]

    gate = conv(g_ref, pg_ref, wg_ref, bg_ref)
    val = conv(v_ref, pv_ref, wv_ref, bv_ref)
    o_ref[0] = (gate * jax.nn.sigmoid(gate) * val).astype(o_ref.dtype)


def _conv3_gate(up3, w_dw, b_dw, *, ts, tc, name):
    bsz, slen, two_h = up3.shape
    hidden = two_h // 2
    kw = w_dw.shape[0]
    assert kw == 3
    nj = hidden // tc
    per_halo = ts // _FFN_HALO
    prev_spec = lambda off: pl.BlockSpec(
        (1, _FFN_HALO, tc), lambda b, i, j: (b, jnp.maximum(i * per_halo - 1, 0), j + off))
    cur_spec = lambda off: pl.BlockSpec((1, ts, tc), lambda b, i, j: (b, i, j + off))
    par_spec = lambda r, off: pl.BlockSpec((r, tc), lambda b, i, j: (0, j + off))
    w = w_dw.astype(F32)
    bias = b_dw.astype(F32).reshape(1, two_h)
    return pl.pallas_call(
        _conv3_gate_kernel,
        grid=(bsz, slen // ts, nj),
        in_specs=[prev_spec(0), prev_spec(nj), cur_spec(0), cur_spec(nj),
                  par_spec(kw, 0), par_spec(kw, nj), par_spec(1, 0), par_spec(1, nj)],
        out_specs=pl.BlockSpec((1, ts, tc), lambda b, i, j: (b, i, j)),
        out_shape=jax.ShapeDtypeStruct((bsz, slen, hidden), BF16),
        compiler_params=_cparams(3),
        name=name,
    )(up3, up3, up3, up3, w, w, bias, bias)


def _rmsnorm_kernel(x_ref, g_ref, o_ref):
    x = x_ref[...]
    ms = jnp.mean(x * x, axis=-1, keepdims=True)
    o_ref[...] = x * lax.rsqrt(ms + EPS) * g_ref[...]


def _rmsnorm(x, g, *, tm, name):
    m, k = x.shape
    return pl.pallas_call(
        _rmsnorm_kernel,
        grid=(m // tm,),
        in_specs=[pl.BlockSpec((tm, k), lambda i: (i, 0)),
                  pl.BlockSpec((1, k), lambda i: (0, 0))],
        out_specs=pl.BlockSpec((tm, k), lambda i: (i, 0)),
        out_shape=jax.ShapeDtypeStruct((m, k), F32),
        compiler_params=_cparams(1),
        name=name,
    )(x, g.astype(F32).reshape(1, k))


def kernel(x, rel_bias, norm_mix, norm_ffn, norm_final, ab_w_in, ab_w_out, diff_lq1, diff_lk1, diff_lq2, diff_lk2, diff_head_norm, s5_lambda_re, s5_lambda_im, s5_log_step, s5_b_re, s5_b_im, s5_c_re, s5_c_im, s5_d, s5_w_glu, s5_b_glu, conv_w_pw1, conv_w_dw, conv_b_dw, conv_ln_g, conv_ln_b, conv_w_pw2, ffn_w_up, ffn_w_dw, ffn_b_dw, ffn_w_down):
    bsz, slen, d_model = x.shape
    depth = norm_mix.shape[0]
    m = bsz * slen
    att_width = N_HEADS * HEAD_WIDTH
    period = slen // SSM_CHUNK
    n_levels = period.bit_length() - 1
    assert (1 << n_levels) == period and slen % SSM_CHUNK == 0

    xf = x.astype(F32).reshape(m, d_model)
    for layer in range(depth):
        i = layer // 2
        g_mix = norm_mix[layer].astype(F32)
        if layer % 2 == 0:
            proj = _norm_matmul(xf, g_mix, ab_w_in[i].astype(BF16), tm=1024, tn=1024,
                                name=f"in_proj_{layer}")
            lam_init = 0.8 - 0.6 * math.exp(-0.3 * layer)
            att = _diff_attention(proj.reshape(bsz, slen, -1), rel_bias, diff_lq1[i], diff_lk1[i],
                                  diff_lq2[i], diff_lk2[i], diff_head_norm[i],
                                  lam_init=lam_init, t=256, name=f"diff_attn_{layer}")
            ops = _s5_operators(s5_lambda_re[i], s5_lambda_im[i], s5_log_step[i], s5_b_re[i],
                                s5_b_im[i], s5_c_re[i], s5_c_im[i], s5_d[i], n_levels)
            g_act = _s5_scan(proj[:, 3 * att_width:], ops, bsz=bsz, slen=slen, name=f"s5_{layer}")
            ssm = _glu_matmul(g_act, s5_w_glu[i].astype(BF16), s5_b_glu[i].astype(F32),
                              tm=1024, name=f"s5_glu_{layer}")
            w_out = ab_w_out[i].astype(BF16)
            xf = _matmul_residual([(att.reshape(m, att_width), w_out[:att_width]),
                                   (ssm, w_out[att_width:])], xf, tm=1024, tn=1024,
                                  name=f"out_proj_{layer}")
        else:
            z = _norm_matmul(xf, g_mix, conv_w_pw1[i].astype(BF16), glu=True, tm=512, tn=1024,
                             name=f"conv_pw1_{layer}")
            c = _conv_ln_silu(z.reshape(bsz, slen, d_model), conv_w_dw[i], conv_b_dw[i],
                              conv_ln_g[i], conv_ln_b[i], ts=256, name=f"conv_dw_{layer}")
            xf = _matmul_residual([(c.reshape(m, d_model), conv_w_pw2[i].astype(BF16))], xf,
                                  tm=1024, tn=1024, name=f"conv_pw2_{layer}")
        up = _norm_matmul(xf, norm_ffn[layer].astype(F32), ffn_w_up[layer].astype(BF16),
                          tm=1024, tn=1024, name=f"ffn_up_{layer}")
        act = _conv3_gate(up.reshape(bsz, slen, -1), ffn_w_dw[layer], ffn_b_dw[layer],
                          ts=512, tc=512, name=f"ffn_gate_{layer}")
        xf = _matmul_residual([(act.reshape(m, -1), ffn_w_down[layer].astype(BF16))], xf,
                              tm=512, tn=1024, name=f"ffn_down_{layer}")
    out = _rmsnorm(xf, norm_final, tm=512, name="final_norm")
    return out.reshape(bsz, slen, d_model).astype(x.dtype)
```

```python
import functools
import math

import numpy as np
import jax
import jax.numpy as jnp
from jax import lax
from jax.experimental import pallas as pl
from jax.experimental.pallas import tpu as pltpu

F32 = jnp.float32
BF16 = jnp.bfloat16

N_HEADS = 8
HEAD_DIM = 64
HEAD_WIDTH = 2 * HEAD_DIM
N_BUCKETS = 32
MAX_DISTANCE = 128
SSM_GROUP = 16
SSM_STATE = 64
SSM_CHUNK = 16
EPS = 1e-6

V7X_VMEM_LIMIT_BYTES = 56 * 1024 * 1024
MASK_VALUE = -1e30
LOG2E = math.log2(math.e)

_NT_DIMS = (((1,), (1,)), ((), ()))


def _cparams(n_axes):
    return pltpu.CompilerParams(
        dimension_semantics=("arbitrary",) * n_axes,
        vmem_limit_bytes=V7X_VMEM_LIMIT_BYTES)


def _dot(a, b):
    return jnp.dot(a, b, preferred_element_type=F32)


def _norm_mm_kernel(x_ref, g_ref, *rest, glu):
    if glu:
        wa_ref, wb_ref, o_ref, h_ref = rest
    else:
        w_ref, o_ref, h_ref = rest

    @pl.when(pl.program_id(1) == 0)
    def _():
        x = x_ref[...]
        ms = jnp.mean(x * x, axis=-1, keepdims=True)
        h_ref[...] = (x * lax.rsqrt(ms + EPS) * g_ref[...]).astype(h_ref.dtype)

    h = h_ref[...]
    if glu:
        a = _dot(h, wa_ref[...])
        gate = _dot(h, wb_ref[...])
        o_ref[...] = (a * jax.nn.sigmoid(gate)).astype(o_ref.dtype)
    else:
        o_ref[...] = _dot(h, w_ref[...]).astype(o_ref.dtype)


def _norm_matmul(x, g, w, *, glu=False, tm, tn, name):
    m, k = x.shape
    n = w.shape[1] // 2 if glu else w.shape[1]
    in_specs = [pl.BlockSpec((tm, k), lambda i, j: (i, 0)),
                pl.BlockSpec((1, k), lambda i, j: (0, 0)),
                pl.BlockSpec((k, tn), lambda i, j: (0, j))]
    args = [x, g.reshape(1, k), w]
    if glu:
        in_specs.append(pl.BlockSpec((k, tn), lambda i, j: (0, j + n // tn)))
        args.append(w)
    return pl.pallas_call(
        functools.partial(_norm_mm_kernel, glu=glu),
        grid=(m // tm, n // tn),
        in_specs=in_specs,
        out_specs=pl.BlockSpec((tm, tn), lambda i, j: (i, j)),
        out_shape=jax.ShapeDtypeStruct((m, n), BF16),
        scratch_shapes=[pltpu.VMEM((tm, k), BF16)],
        compiler_params=_cparams(2),
        name=name,
    )(*args)


def _mm_res_kernel(*refs, n_pairs):
    res_ref, o_ref = refs[2 * n_pairs], refs[2 * n_pairs + 1]
    acc = res_ref[...]
    for p in range(n_pairs):
        acc = acc + _dot(refs[2 * p][...], refs[2 * p + 1][...])
    o_ref[...] = acc


def _matmul_residual(pairs, res, *, tm, tn, name):
    m, n = res.shape
    in_specs, args = [], []
    for a, w in pairs:
        k = a.shape[1]
        in_specs += [pl.BlockSpec((tm, k), lambda i, j: (i, 0)),
                     pl.BlockSpec((k, tn), lambda i, j: (0, j))]
        args += [a, w]
    in_specs.append(pl.BlockSpec((tm, tn), lambda i, j: (i, j)))
    args.append(res)
    return pl.pallas_call(
        functools.partial(_mm_res_kernel, n_pairs=len(pairs)),
        grid=(m // tm, n // tn),
        in_specs=in_specs,
        out_specs=pl.BlockSpec((tm, tn), lambda i, j: (i, j)),
        out_shape=jax.ShapeDtypeStruct((m, n), F32),
        compiler_params=_cparams(2),
        name=name,
    )(*args)


def _glu_mm_kernel(g_ref, w_ref, b_ref, o_ref):
    g = g_ref[...]
    z = _dot(g, w_ref[...]) + b_ref[...]
    o_ref[...] = (g.astype(F32) * jax.nn.sigmoid(z)).astype(o_ref.dtype)


def _glu_matmul(g, w, b, *, tm, name):
    m, k = g.shape
    n = w.shape[1]
    return pl.pallas_call(
        _glu_mm_kernel,
        grid=(m // tm,),
        in_specs=[pl.BlockSpec((tm, k), lambda i: (i, 0)),
                  pl.BlockSpec((k, n), lambda i: (0, 0)),
                  pl.BlockSpec((1, n), lambda i: (0, 0))],
        out_specs=pl.BlockSpec((tm, n), lambda i: (i, 0)),
        out_shape=jax.ShapeDtypeStruct((m, n), BF16),
        compiler_params=_cparams(1),
        name=name,
    )(g, w, b.reshape(1, n))


def _t5_bucket_np(rel):
    n = np.maximum(rel, 0)
    max_exact = N_BUCKETS // 2
    nf = np.maximum(n, 1).astype(np.float32)
    large = max_exact + (np.log(nf / np.float32(max_exact))
                         / np.float32(math.log(MAX_DISTANCE / max_exact))
                         * np.float32(N_BUCKETS - max_exact)).astype(np.int32)
    large = np.minimum(large, N_BUCKETS - 1)
    return np.where(n < max_exact, n, large)


def _bias_tables(rel_bias, slen, t):
    i = np.arange(t)[:, None]
    j = np.arange(t)[None, :]
    far = _t5_bucket_np(np.arange(t + 1, max(slen, t + 2)))
    far_bucket = int(far[0])
    assert np.all(far == far_bucket)
    rb = rel_bias.astype(F32)
    rb = (rb - rb[far_bucket]) * LOG2E
    d0 = jnp.where(jnp.asarray(i >= j)[..., None], rb[_t5_bucket_np(i - j)], MASK_VALUE)
    d1 = rb[_t5_bucket_np(i - j + t)]
    return jnp.stack([d0, d1], axis=0).transpose(3, 0, 1, 2)


def _attn_kernel(q_ref, k_ref, v_ref, d_ref, lq1_ref, lk1_ref, lq2_ref, lk2_ref, hn_ref, o_ref,
                 vaug_ref, acc_ref, m_ref, *, t, lam_init):
    qi = pl.program_id(2)

    @pl.when(qi == 0)
    def _():
        vaug_ref[:, :HEAD_WIDTH] = v_ref[0]
        vaug_ref[:, HEAD_WIDTH:] = jnp.ones((vaug_ref.shape[0], HEAD_WIDTH), vaug_ref.dtype)

    q = q_ref[0]
    lane = lax.broadcasted_iota(jnp.int32, q.shape, 1)
    zero = jnp.zeros_like(q)
    q_halves = (jnp.where(lane < HEAD_DIM, q, zero), jnp.where(lane >= HEAD_DIM, q, zero))

    acc_ref[...] = jnp.zeros_like(acc_ref)
    m_ref[...] = jnp.full_like(m_ref, MASK_VALUE)

    def block(start, bias):
        kb = k_ref[0, pl.ds(start, t), :]
        vb = vaug_ref[pl.ds(start, t), :]
        for idx in range(2):
            s = lax.dot_general(q_halves[idx], kb, _NT_DIMS, preferred_element_type=F32)
            if bias is not None:
                s = s + bias
            m_old = m_ref[idx]
            m_new = jnp.maximum(m_old, jnp.max(s, axis=-1, keepdims=True))
            alpha = jnp.exp2(m_old - m_new)
            p = jnp.exp2(s - jnp.concatenate([m_new] * (t // HEAD_WIDTH), axis=1))
            pv = _dot(p.astype(vb.dtype), vb)
            acc_ref[idx] = jnp.concatenate([alpha, alpha], axis=1) * acc_ref[idx] + pv
            m_ref[idx] = m_new

    def far_body(ki, carry):
        block(pl.multiple_of(ki * t, t), None)
        return carry

    lax.fori_loop(0, qi - 1, far_body, 0)

    @pl.when(qi >= 1)
    def _():
        block(pl.multiple_of((qi - 1) * t, t), d_ref[0, 1])

    block(pl.multiple_of(qi * t, t), d_ref[0, 0])

    lam = (jnp.exp(jnp.sum(lq1_ref[...] * lk1_ref[...], keepdims=True))
           - jnp.exp(jnp.sum(lq2_ref[...] * lk2_ref[...], keepdims=True)) + lam_init)
    a1 = acc_ref[0]
    a2 = acc_ref[1]
    o = a1[:, :HEAD_WIDTH] / a1[:, HEAD_WIDTH:] - lam * (a2[:, :HEAD_WIDTH] / a2[:, HEAD_WIDTH:])
    ms = jnp.mean(o * o, axis=-1, keepdims=True)
    o = o * lax.rsqrt(ms + EPS) * hn_ref[...] * (1.0 - lam_init)
    o_ref[0] = o.astype(o_ref.dtype)


def _diff_attention(proj3, rel_bias, lq1, lk1, lq2, lk2, head_norm, *, lam_init, t, name):
    bsz, slen, _ = proj3.shape
    tiles = _bias_tables(rel_bias, slen, t)
    vec = lambda a: a.astype(F32).reshape(1, -1)
    small = lambda n: pl.BlockSpec((1, n), lambda b, h, qi: (0, 0))
    return pl.pallas_call(
        functools.partial(_attn_kernel, t=t, lam_init=lam_init),
        grid=(bsz, N_HEADS, slen // t),
        in_specs=[pl.BlockSpec((1, t, HEAD_WIDTH), lambda b, h, qi: (b, qi, h)),
                  pl.BlockSpec((1, slen, HEAD_WIDTH), lambda b, h, qi: (b, 0, N_HEADS + h)),
                  pl.BlockSpec((1, slen, HEAD_WIDTH), lambda b, h, qi: (b, 0, 2 * N_HEADS + h)),
                  pl.BlockSpec((1, 2, t, t), lambda b, h, qi: (h, 0, 0, 0)),
                  small(HEAD_DIM), small(HEAD_DIM), small(HEAD_DIM), small(HEAD_DIM),
                  small(HEAD_WIDTH)],
        out_specs=pl.BlockSpec((1, t, HEAD_WIDTH), lambda b, h, qi: (b, qi, h)),
        out_shape=jax.ShapeDtypeStruct((bsz, slen, N_HEADS * HEAD_WIDTH), BF16),
        scratch_shapes=[pltpu.VMEM((slen, 2 * HEAD_WIDTH), BF16),
                        pltpu.VMEM((2, t, 2 * HEAD_WIDTH), F32),
                        pltpu.VMEM((2, t, HEAD_WIDTH), F32)],
        compiler_params=_cparams(3),
        name=name,
    )(proj3, proj3, proj3, tiles, vec(lq1), vec(lk1), vec(lq2), vec(lk2), vec(head_norm))


def _s5_operators(lre, lim, log_step, bre, bim, cre, cim, dd, n_levels):
    f = lambda a: a.astype(F32)
    lre, lim, bre, bim, cre, cim = map(f, (lre, lim, bre, bim, cre, cim))
    n_groups = lre.shape[0]
    cl = SSM_CHUNK
    hi = lax.Precision.HIGHEST
    dt = jnp.exp(f(log_step))[:, None]
    zr, zi = lre * dt, lim * dt
    steps = jnp.arange(cl + 1, dtype=F32)[None, :, None]
    mag = jnp.exp(zr[:, None, :] * steps)
    ang = zi[:, None, :] * steps
    pr, pi = mag * jnp.cos(ang), mag * jnp.sin(ang)
    nr = jnp.expm1(zr) * jnp.cos(zi) - 2.0 * jnp.sin(0.5 * zi) ** 2
    ni = jnp.exp(zr) * jnp.sin(zi)
    den = lre * lre + lim * lim
    fr, fi = (nr * lre + ni * lim) / den, (ni * lre - nr * lim) / den
    bbr = fr[..., None] * bre - fi[..., None] * bim
    bbi = fr[..., None] * bim + fi[..., None] * bre
    cpr = cre[:, None] * pr[:, :, None, :] - cim[:, None] * pi[:, :, None, :]
    cpi = cre[:, None] * pi[:, :, None, :] + cim[:, None] * pr[:, :, None, :]
    kern = (jnp.einsum('gjhp,gpi->gjhi', cpr[:, :cl], bbr, precision=hi)
            - jnp.einsum('gjhp,gpi->gjhi', cpi[:, :cl], bbi, precision=hi))
    s_idx = np.arange(cl)[:, None]
    t_idx = np.arange(cl)[None, :]
    lag = np.clip(t_idx - s_idx, 0, cl - 1)
    toep = jnp.where(jnp.asarray(t_idx >= s_idx)[None, :, :, None, None], kern[:, lag], 0.0)
    toep = toep.transpose(0, 1, 4, 2, 3).reshape(n_groups, cl * SSM_GROUP, cl * SSM_GROUP)
    rev = np.arange(cl - 1, -1, -1)
    wr = pr[:, rev][:, :, None, :] * bbr.transpose(0, 2, 1)[:, None] \
        - pi[:, rev][:, :, None, :] * bbi.transpose(0, 2, 1)[:, None]
    wi = pr[:, rev][:, :, None, :] * bbi.transpose(0, 2, 1)[:, None] \
        + pi[:, rev][:, :, None, :] * bbr.transpose(0, 2, 1)[:, None]
    w_in = jnp.concatenate([wr, wi], axis=-1).reshape(n_groups, cl * SSM_GROUP, 2 * SSM_STATE)
    v_re = cpr[:, 1:].transpose(0, 3, 1, 2).reshape(n_groups, SSM_STATE, cl * SSM_GROUP)
    v_im = -cpi[:, 1:].transpose(0, 3, 1, 2).reshape(n_groups, SSM_STATE, cl * SSM_GROUP)
    v_out = jnp.concatenate([v_re, v_im], axis=1)
    ar, ai = pr[:, cl], pi[:, cl]
    ars, ais = [], []
    for _ in range(n_levels):
        ars.append(jnp.concatenate([ar, ar], axis=-1))
        ais.append(jnp.concatenate([-ai, ai], axis=-1))
        ar, ai = ar * ar - ai * ai, 2.0 * ar * ai
    d_row = jnp.tile(f(dd), (1, cl))[:, None, :]
    return (toep.astype(BF16), w_in.astype(BF16), v_out.astype(BF16),
            jnp.stack(ars, axis=1), jnp.stack(ais, axis=1), d_row)


def _s5_kernel(u_ref, t_ref, w_ref, v_ref, ar_ref, ai_ref, d_ref, o_ref, *, n_levels, period):
    u = u_ref[0]
    x = _dot(u, w_ref[0])
    row = lax.rem(lax.broadcasted_iota(jnp.int32, x.shape, 0), period)
    for k in range(n_levels):
        d = 1 << k
        xs = jnp.where(row >= d, pltpu.roll(x, d, axis=0), 0.0)
        x = x + ar_ref[0, k:k + 1, :] * xs + ai_ref[0, k:k + 1, :] * pltpu.roll(xs, SSM_STATE, axis=1)
    xp = jnp.where(row >= 1, pltpu.roll(x, 1, axis=0), 0.0)
    xh = xp.astype(BF16)
    xl = (xp - xh.astype(F32)).astype(BF16)
    v = v_ref[0]
    y = _dot(u, t_ref[0]) + _dot(xh, v) + _dot(xl, v) + d_ref[0] * u.astype(F32)
    o_ref[0] = jax.nn.gelu(y).astype(o_ref.dtype)


def _s5_scan(u, ops, *, bsz, slen, name):
    toep, w_in, v_out, ars, ais, d_row = ops
    n_groups = toep.shape[0]
    cl = SSM_CHUNK
    period = slen // cl
    rows = bsz * period
    n_levels = ars.shape[1]
    width = cl * SSM_GROUP
    uc = u.reshape(rows, cl, n_groups, SSM_GROUP).transpose(2, 0, 1, 3).reshape(n_groups, rows, width)
    per_group = lambda *shape: pl.BlockSpec((1,) + shape, lambda g: (g, 0, 0))
    y = pl.pallas_call(
        functools.partial(_s5_kernel, n_levels=n_levels, period=period),
        grid=(n_groups,),
        in_specs=[per_group(rows, width), per_group(width, width),
                  per_group(width, 2 * SSM_STATE), per_group(2 * SSM_STATE, width),
                  per_group(n_levels, 2 * SSM_STATE), per_group(n_levels, 2 * SSM_STATE),
                  per_group(1, width)],
        out_specs=per_group(rows, width),
        out_shape=jax.ShapeDtypeStruct((n_groups, rows, width), BF16),
        compiler_params=_cparams(1),
        name=name,
    )(uc, toep, w_in, v_out, ars, ais, d_row)
    return y.reshape(n_groups, rows, cl, SSM_GROUP).transpose(1, 2, 0, 3).reshape(bsz * slen, -1)


_CONV_HALO = 32
_CONV_ROWS = 32
_CONV_COLS = 512


def _conv_ln_kernel(prev_ref, cur_ref, w_ref, b_ref, lg_ref, lb_ref, o_ref, ext_ref, y_ref, *, ts):
    kw = w_ref.shape[0]
    n_ch = cur_ref.shape[-1]
    first = pl.program_id(1) == 0
    prev = prev_ref[0].astype(F32)
    ext_ref[0:_CONV_HALO, :] = jnp.where(first, jnp.zeros_like(prev), prev)
    ext_ref[_CONV_HALO:, :] = cur_ref[0].astype(F32)
    base = _CONV_HALO - (kw - 1)
    for c0 in range(0, n_ch, _CONV_COLS):
        cols = slice(c0, c0 + _CONV_COLS)
        for r0 in range(0, ts, _CONV_ROWS):
            acc = jnp.broadcast_to(b_ref[:, cols], (_CONV_ROWS, _CONV_COLS))
            for k in range(kw):
                acc = acc + w_ref[k:k + 1, cols] * ext_ref[r0 + base + k:r0 + base + k + _CONV_ROWS, cols]
            y_ref[r0:r0 + _CONV_ROWS, cols] = acc
    y = y_ref[...]
    mu = jnp.mean(y, axis=-1, keepdims=True)
    yc = y - mu
    var = jnp.mean(yc * yc, axis=-1, keepdims=True)
    r = yc * lax.rsqrt(var + EPS) * lg_ref[...] + lb_ref[...]
    o_ref[0] = (r * jax.nn.sigmoid(r)).astype(o_ref.dtype)


def _conv_ln_silu(z3, w_dw, b_dw, ln_g, ln_b, *, ts, name):
    bsz, slen, n_ch = z3.shape
    kw = w_dw.shape[0]
    assert kw - 1 <= _CONV_HALO
    row = lambda a: a.astype(F32).reshape(1, n_ch)
    full = lambda r: pl.BlockSpec((r, n_ch), lambda b, i: (0, 0))
    per_halo = ts // _CONV_HALO
    return pl.pallas_call(
        functools.partial(_conv_ln_kernel, ts=ts),
        grid=(bsz, slen // ts),
        in_specs=[pl.BlockSpec((1, _CONV_HALO, n_ch),
                               lambda b, i: (b, jnp.maximum(i * per_halo - 1, 0), 0)),
                  pl.BlockSpec((1, ts, n_ch), lambda b, i: (b, i, 0)),
                  full(kw), full(1), full(1), full(1)],
        out_specs=pl.BlockSpec((1, ts, n_ch), lambda b, i: (b, i, 0)),
        out_shape=jax.ShapeDtypeStruct((bsz, slen, n_ch), BF16),
        scratch_shapes=[pltpu.VMEM((ts + _CONV_HALO, n_ch), F32),
                        pltpu.VMEM((ts, n_ch), F32)],
        compiler_params=_cparams(2),
        name=name,
    )(z3, z3, w_dw.astype(F32), row(b_dw), row(ln_g), row(ln_b))


_FFN_HALO = 16


def _conv3_gate_kernel(pg_ref, pv_ref, g_ref, v_ref, wg_ref, wv_ref, bg_ref, bv_ref, o_ref):
    first = pl.program_id(1) == 0

    def conv(cur_ref, prev_ref, w_ref, b_ref):
        cur = cur_ref[0].astype(F32)
        prev = prev_ref[0].astype(F32)
        prev = jnp.where(first, jnp.zeros_like(prev), prev)
        p1 = prev[_FFN_HALO - 1:_FFN_HALO]
        p2 = prev[_FFN_HALO - 2:_FFN_HALO - 1]
        row = lax.broadcasted_iota(jnp.int32, cur.shape, 0)
        x1 = jnp.where(row == 0, p1, pltpu.roll(cur, 1, axis=0))
        x2 = jnp.where(row == 0, p2, jnp.where(row == 1, p1, pltpu.roll(cur, 2, axis=0)))
        return w_ref[0:1] * x2 + w_ref[1:2] * x1 + w_ref[2:3] * cur + b_ref[...]

    gate = conv(g_ref, pg_ref, wg_ref, bg_ref)
    val = conv(v_ref, pv_ref, wv_ref, bv_ref)
    o_ref[0] = (gate * jax.nn.sigmoid(gate) * val).astype(o_ref.dtype)


def _conv3_gate(up3, w_dw, b_dw, *, ts, tc, name):
    bsz, slen, two_h = up3.shape
    hidden = two_h // 2
    kw = w_dw.shape[0]
    assert kw == 3
    nj = hidden // tc
    per_halo = ts // _FFN_HALO
    prev_spec = lambda off: pl.BlockSpec(
        (1, _FFN_HALO, tc), lambda b, i, j: (b, jnp.maximum(i * per_halo - 1, 0), j + off))
    cur_spec = lambda off: pl.BlockSpec((1, ts, tc), lambda b, i, j: (b, i, j + off))
    par_spec = lambda r, off: pl.BlockSpec((r, tc), lambda b, i, j: (0, j + off))
    w = w_dw.astype(F32)
    bias = b_dw.astype(F32).reshape(1, two_h)
    return pl.pallas_call(
        _conv3_gate_kernel,
        grid=(bsz, slen // ts, nj),
        in_specs=[prev_spec(0), prev_spec(nj), cur_spec(0), cur_spec(nj),
                  par_spec(kw, 0), par_spec(kw, nj), par_spec(1, 0), par_spec(1, nj)],
        out_specs=pl.BlockSpec((1, ts, tc), lambda b, i, j: (b, i, j)),
        out_shape=jax.ShapeDtypeStruct((bsz, slen, hidden), BF16),
        compiler_params=_cparams(3),
        name=name,
    )(up3, up3, up3, up3, w, w, bias, bias)


def _rmsnorm_kernel(x_ref, g_ref, o_ref):
    x = x_ref[...]
    ms = jnp.mean(x * x, axis=-1, keepdims=True)
    o_ref[...] = x * lax.rsqrt(ms + EPS) * g_ref[...]


def _rmsnorm(x, g, *, tm, name):
    m, k = x.shape
    return pl.pallas_call(
        _rmsnorm_kernel,
        grid=(m // tm,),
        in_specs=[pl.BlockSpec((tm, k), lambda i: (i, 0)),
                  pl.BlockSpec((1, k), lambda i: (0, 0))],
        out_specs=pl.BlockSpec((tm, k), lambda i: (i, 0)),
        out_shape=jax.ShapeDtypeStruct((m, k), F32),
        compiler_params=_cparams(1),
        name=name,
    )(x, g.astype(F32).reshape(1, k))


def kernel(x, rel_bias, norm_mix, norm_ffn, norm_final, ab_w_in, ab_w_out, diff_lq1, diff_lk1, diff_lq2, diff_lk2, diff_head_norm, s5_lambda_re, s5_lambda_im, s5_log_step, s5_b_re, s5_b_im, s5_c_re, s5_c_im, s5_d, s5_w_glu, s5_b_glu, conv_w_pw1, conv_w_dw, conv_b_dw, conv_ln_g, conv_ln_b, conv_w_pw2, ffn_w_up, ffn_w_dw, ffn_b_dw, ffn_w_down):
    bsz, slen, d_model = x.shape
    depth = norm_mix.shape[0]
    m = bsz * slen
    att_width = N_HEADS * HEAD_WIDTH
    period = slen // SSM_CHUNK
    n_levels = period.bit_length() - 1
    assert (1 << n_levels) == period and slen % SSM_CHUNK == 0

    xf = x.astype(F32).reshape(m, d_model)
    for layer in range(depth):
        i = layer // 2
        g_mix = norm_mix[layer].astype(F32)
        if layer % 2 == 0:
            col_scale = jnp.where(jnp.arange(ab_w_in.shape[-1]) < att_width,
                                  LOG2E * HEAD_DIM ** -0.5, 1.0).astype(F32)
            w_in = (ab_w_in[i].astype(F32) * col_scale).astype(BF16)
            proj = _norm_matmul(xf, g_mix, w_in, tm=1024, tn=1024, name=f"in_proj_{layer}")
            lam_init = 0.8 - 0.6 * math.exp(-0.3 * layer)
            att = _diff_attention(proj.reshape(bsz, slen, -1), rel_bias, diff_lq1[i], diff_lk1[i],
                                  diff_lq2[i], diff_lk2[i], diff_head_norm[i],
                                  lam_init=lam_init, t=512, name=f"diff_attn_{layer}")
            ops = _s5_operators(s5_lambda_re[i], s5_lambda_im[i], s5_log_step[i], s5_b_re[i],
                                s5_b_im[i], s5_c_re[i], s5_c_im[i], s5_d[i], n_levels)
            g_act = _s5_scan(proj[:, 3 * att_width:], ops, bsz=bsz, slen=slen, name=f"s5_{layer}")
            ssm = _glu_matmul(g_act, s5_w_glu[i].astype(BF16), s5_b_glu[i].astype(F32),
                              tm=1024, name=f"s5_glu_{layer}")
            w_out = ab_w_out[i].astype(BF16)
            xf = _matmul_residual([(att.reshape(m, att_width), w_out[:att_width]),
                                   (ssm, w_out[att_width:])], xf, tm=1024, tn=1024,
                                  name=f"out_proj_{layer}")
        else:
            z = _norm_matmul(xf, g_mix, conv_w_pw1[i].astype(BF16), glu=True, tm=512, tn=1024,
                             name=f"conv_pw1_{layer}")
            c = _conv_ln_silu(z.reshape(bsz, slen, d_model), conv_w_dw[i], conv_b_dw[i],
                              conv_ln_g[i], conv_ln_b[i], ts=256, name=f"conv_dw_{layer}")
            xf = _matmul_residual([(c.reshape(m, d_model), conv_w_pw2[i].astype(BF16))], xf,
                                  tm=1024, tn=1024, name=f"conv_pw2_{layer}")
        up = _norm_matmul(xf, norm_ffn[layer].astype(F32), ffn_w_up[layer].astype(BF16),
                          tm=1024, tn=1024, name=f"ffn_up_{layer}")
        act = _conv3_gate(up.reshape(bsz, slen, -1), ffn_w_dw[layer], ffn_b_dw[layer],
                          ts=512, tc=512, name=f"ffn_gate_{layer}")
        xf = _matmul_residual([(act.reshape(m, -1), ffn_w_down[layer].astype(BF16))], xf,
                              tm=512, tn=1024, name=f"ffn_down_{layer}")
    out = _rmsnorm(xf, norm_final, tm=512, name="final_norm")
    return out.reshape(bsz, slen, d_model).astype(x.dtype)
```

```python
import functools
import math

import numpy as np
import jax
import jax.numpy as jnp
from jax import lax
from jax.experimental import pallas as pl
from jax.experimental.pallas import tpu as pltpu

F32 = jnp.float32
BF16 = jnp.bfloat16

N_HEADS = 8
HEAD_DIM = 64
HEAD_WIDTH = 2 * HEAD_DIM
N_BUCKETS = 32
MAX_DISTANCE = 128
SSM_GROUP = 16
SSM_STATE = 64
SSM_CHUNK = 16
EPS = 1e-6

V7X_LANES = 128
V7X_SUBLANES = 8
V7X_VMEM_LIMIT_BYTES = 56 * 1024 * 1024
MASK_VALUE = -1e30
LOG2E = math.log2(math.e)

_NT_DIMS = (((1,), (1,)), ((), ()))


def _cparams(n_axes):
    return pltpu.CompilerParams(
        dimension_semantics=("arbitrary",) * n_axes,
        vmem_limit_bytes=V7X_VMEM_LIMIT_BYTES)


def _dot(a, b):
    return jnp.dot(a, b, preferred_element_type=F32)


def _rms_scale(x, g):
    ms = jnp.mean(x * x, axis=-1, keepdims=True)
    return x * lax.rsqrt(ms + EPS) * g


def _static_lookup(table, idx):
    idx = np.asarray(idx)
    values = sorted(set(idx.ravel().tolist()))
    col = lambda b: table[b].reshape((-1,) + (1,) * idx.ndim)
    out = jnp.broadcast_to(col(values[-1]), (table.shape[1],) + idx.shape)
    for b in values[-2::-1]:
        out = jnp.where(jnp.asarray(idx == b)[None], col(b), out)
    return out


def _norm_mm_kernel(x_ref, g_ref, *rest, glu):
    if glu:
        wa_ref, wb_ref, o_ref, h_ref = rest
    else:
        w_ref, o_ref, h_ref = rest

    @pl.when(pl.program_id(1) == 0)
    def _():
        h_ref[...] = _rms_scale(x_ref[...], g_ref[...]).astype(h_ref.dtype)

    h = h_ref[...]
    if glu:
        a = _dot(h, wa_ref[...])
        gate = _dot(h, wb_ref[...])
        o_ref[...] = (a * jax.nn.sigmoid(gate)).astype(o_ref.dtype)
    else:
        o_ref[...] = _dot(h, w_ref[...]).astype(o_ref.dtype)


def _norm_matmul(x, g, w, *, glu=False, out_dtype=BF16, tm, tn, name):
    m, k = x.shape
    n = w.shape[1] // 2 if glu else w.shape[1]
    in_specs = [pl.BlockSpec((tm, k), lambda i, j: (i, 0)),
                pl.BlockSpec((1, k), lambda i, j: (0, 0)),
                pl.BlockSpec((k, tn), lambda i, j: (0, j))]
    args = [x, g.reshape(1, k), w]
    if glu:
        in_specs.append(pl.BlockSpec((k, tn), lambda i, j: (0, j + n // tn)))
        args.append(w)
    return pl.pallas_call(
        functools.partial(_norm_mm_kernel, glu=glu),
        grid=(m // tm, n // tn),
        in_specs=in_specs,
        out_specs=pl.BlockSpec((tm, tn), lambda i, j: (i, j)),
        out_shape=jax.ShapeDtypeStruct((m, n), out_dtype),
        scratch_shapes=[pltpu.VMEM((tm, k), BF16)],
        compiler_params=_cparams(2),
        name=name,
    )(*args)


def _mm_res_kernel(*refs, n_pairs):
    res_ref, o_ref = refs[2 * n_pairs], refs[2 * n_pairs + 1]
    acc = res_ref[...]
    for p in range(n_pairs):
        acc = acc + _dot(refs[2 * p][...], refs[2 * p + 1][...])
    o_ref[...] = acc


def _matmul_residual(pairs, res, *, tm, tn, name):
    m, n = res.shape
    in_specs, args = [], []
    for a, w in pairs:
        k = a.shape[1]
        in_specs += [pl.BlockSpec((tm, k), lambda i, j: (i, 0)),
                     pl.BlockSpec((k, tn), lambda i, j: (0, j))]
        args += [a, w]
    in_specs.append(pl.BlockSpec((tm, tn), lambda i, j: (i, j)))
    args.append(res)
    return pl.pallas_call(
        functools.partial(_mm_res_kernel, n_pairs=len(pairs)),
        grid=(m // tm, n // tn),
        in_specs=in_specs,
        out_specs=pl.BlockSpec((tm, tn), lambda i, j: (i, j)),
        out_shape=jax.ShapeDtypeStruct((m, n), F32),
        compiler_params=_cparams(2),
        name=name,
    )(*args)


def _t5_bucket_np(rel):
    n = np.maximum(rel, 0)
    max_exact = N_BUCKETS // 2
    nf = np.maximum(n, 1).astype(np.float32)
    large = max_exact + (np.log(nf / np.float32(max_exact))
                         / np.float32(math.log(MAX_DISTANCE / max_exact))
                         * np.float32(N_BUCKETS - max_exact)).astype(np.int32)
    large = np.minimum(large, N_BUCKETS - 1)
    return np.where(n < max_exact, n, large)


def _bias_tables(rel_bias, slen, t):
    i = np.arange(t)[:, None]
    j = np.arange(t)[None, :]
    far = _t5_bucket_np(np.arange(t + 1, max(slen, t + 2)))
    far_bucket = int(far[0])
    assert np.all(far == far_bucket)
    rb = rel_bias.astype(F32)
    rb = (rb - rb[far_bucket]) * LOG2E
    d0 = jnp.where(jnp.asarray(i >= j)[None], _static_lookup(rb, _t5_bucket_np(i - j)), MASK_VALUE)
    d1 = _static_lookup(rb, _t5_bucket_np(i - j + t))
    return jnp.stack([d0, d1], axis=1)


def _attn_kernel(q_ref, k_ref, v_ref, d_ref, lq1_ref, lk1_ref, lq2_ref, lk2_ref, hn_ref, o_ref,
                 vaug_ref, acc_ref, m_ref, *, t, lam_init):
    qi = pl.program_id(2)

    @pl.when(qi == 0)
    def _():
        vaug_ref[:, :HEAD_WIDTH] = v_ref[0]
        vaug_ref[:, HEAD_WIDTH:] = jnp.ones((vaug_ref.shape[0], HEAD_WIDTH), vaug_ref.dtype)

    q = q_ref[0]
    lane = lax.broadcasted_iota(jnp.int32, q.shape, 1)
    zero = jnp.zeros_like(q)
    q_halves = (jnp.where(lane < HEAD_DIM, q, zero), jnp.where(lane >= HEAD_DIM, q, zero))

    acc_ref[...] = jnp.zeros_like(acc_ref)
    m_ref[...] = jnp.full_like(m_ref, MASK_VALUE)

    def block(start, bias):
        kb = k_ref[0, pl.ds(start, t), :]
        vb = vaug_ref[pl.ds(start, t), :]
        for idx in range(2):
            s = lax.dot_general(q_halves[idx], kb, _NT_DIMS, preferred_element_type=F32)
            if bias is not None:
                s = s + bias
            m_old = m_ref[idx]
            m_new = jnp.maximum(m_old, jnp.max(s, axis=-1, keepdims=True))
            alpha = jnp.exp2(m_old - m_new)
            p = jnp.exp2(s - jnp.concatenate([m_new] * (t // HEAD_WIDTH), axis=1))
            pv = _dot(p.astype(vb.dtype), vb)
            acc_ref[idx] = jnp.concatenate([alpha, alpha], axis=1) * acc_ref[idx] + pv
            m_ref[idx] = m_new

    def far_body(ki, carry):
        block(pl.multiple_of(ki * t, t), None)
        return carry

    lax.fori_loop(0, qi - 1, far_body, 0)

    @pl.when(qi >= 1)
    def _():
        block(pl.multiple_of((qi - 1) * t, t), d_ref[0, 1])

    block(pl.multiple_of(qi * t, t), d_ref[0, 0])

    lam = (jnp.exp(jnp.sum(lq1_ref[...] * lk1_ref[...], keepdims=True))
           - jnp.exp(jnp.sum(lq2_ref[...] * lk2_ref[...], keepdims=True)) + lam_init)
    a1 = acc_ref[0]
    a2 = acc_ref[1]
    o = a1[:, :HEAD_WIDTH] / a1[:, HEAD_WIDTH:] - lam * (a2[:, :HEAD_WIDTH] / a2[:, HEAD_WIDTH:])
    o_ref[0] = (_rms_scale(o, hn_ref[...]) * (1.0 - lam_init)).astype(o_ref.dtype)


def _diff_attention(qkv3, rel_bias, lq1, lk1, lq2, lk2, head_norm, *, lam_init, t, name):
    bsz, slen, _ = qkv3.shape
    tiles = _bias_tables(rel_bias, slen, t)
    vec = lambda a: a.astype(F32).reshape(1, -1)
    small = lambda n: pl.BlockSpec((1, n), lambda b, h, qi: (0, 0))
    return pl.pallas_call(
        functools.partial(_attn_kernel, t=t, lam_init=lam_init),
        grid=(bsz, N_HEADS, slen // t),
        in_specs=[pl.BlockSpec((1, t, HEAD_WIDTH), lambda b, h, qi: (b, qi, h)),
                  pl.BlockSpec((1, slen, HEAD_WIDTH), lambda b, h, qi: (b, 0, N_HEADS + h)),
                  pl.BlockSpec((1, slen, HEAD_WIDTH), lambda b, h, qi: (b, 0, 2 * N_HEADS + h)),
                  pl.BlockSpec((1, 2, t, t), lambda b, h, qi: (h, 0, 0, 0)),
                  small(HEAD_DIM), small(HEAD_DIM), small(HEAD_DIM), small(HEAD_DIM),
                  small(HEAD_WIDTH)],
        out_specs=pl.BlockSpec((1, t, HEAD_WIDTH), lambda b, h, qi: (b, qi, h)),
        out_shape=jax.ShapeDtypeStruct((bsz, slen, N_HEADS * HEAD_WIDTH), BF16),
        scratch_shapes=[pltpu.VMEM((slen, 2 * HEAD_WIDTH), BF16),
                        pltpu.VMEM((2, t, 2 * HEAD_WIDTH), F32),
                        pltpu.VMEM((2, t, HEAD_WIDTH), F32)],
        compiler_params=_cparams(3),
        name=name,
    )(qkv3, qkv3, qkv3, tiles, vec(lq1), vec(lk1), vec(lq2), vec(lk2), vec(head_norm))


_GROUPS_PER_TILE = V7X_LANES // SSM_GROUP


def _s5_operators(lre, lim, log_step, bre, bim, cre, cim, dd, n_levels):
    f = lambda a: a.astype(F32)
    lre, lim, bre, bim, cre, cim = map(f, (lre, lim, bre, bim, cre, cim))
    n_groups = lre.shape[0]
    cl, gt = SSM_CHUNK, _GROUPS_PER_TILE
    n_tiles = n_groups // gt
    hi = lax.Precision.HIGHEST
    dt = jnp.exp(f(log_step))[:, None]
    zr, zi = lre * dt, lim * dt
    steps = jnp.arange(cl + 1, dtype=F32)[None, :, None]
    mag = jnp.exp(zr[:, None, :] * steps)
    ang = zi[:, None, :] * steps
    pr, pi = mag * jnp.cos(ang), mag * jnp.sin(ang)
    nr = jnp.expm1(zr) * jnp.cos(zi) - 2.0 * jnp.sin(0.5 * zi) ** 2
    ni = jnp.exp(zr) * jnp.sin(zi)
    den = lre * lre + lim * lim
    fr, fi = (nr * lre + ni * lim) / den, (ni * lre - nr * lim) / den
    bbr = fr[..., None] * bre - fi[..., None] * bim
    bbi = fr[..., None] * bim + fi[..., None] * bre
    cpr = cre[:, None] * pr[:, :, None, :] - cim[:, None] * pi[:, :, None, :]
    cpi = cre[:, None] * pi[:, :, None, :] + cim[:, None] * pr[:, :, None, :]
    kern = (jnp.einsum('gjhp,gpi->gjhi', cpr[:, :cl], bbr, precision=hi)
            - jnp.einsum('gjhp,gpi->gjhi', cpi[:, :cl], bbi, precision=hi))
    eye = jnp.eye(gt, dtype=F32)
    tile = lambda a: a.reshape((n_tiles, gt) + a.shape[1:])
    lag_onehot = np.zeros((cl, cl, cl), np.float32)
    for s in range(cl):
        for t in range(s, cl):
            lag_onehot[s, t, t - s] = 1.0
    toep = jnp.einsum('stj,ogjhi,gk->osgitkh', lag_onehot, tile(kern), eye, precision=hi)
    toep = toep.reshape(n_tiles, cl * V7X_LANES, cl * V7X_LANES)
    rev = np.arange(cl - 1, -1, -1)
    prr, pir = pr[:, rev][:, :, None, :], pi[:, rev][:, :, None, :]
    bbr_t, bbi_t = bbr.transpose(0, 2, 1)[:, None], bbi.transpose(0, 2, 1)[:, None]
    w_parts = jnp.stack([prr * bbr_t - pir * bbi_t, prr * bbi_t + pir * bbr_t], axis=3)
    w_in = jnp.einsum('ogsicp,gk->osgickp', tile(w_parts), eye, precision=hi)
    w_in = w_in.reshape(n_tiles, cl * V7X_LANES, 2 * gt * SSM_STATE)
    v_parts = jnp.stack([cpr[:, 1:], -cpi[:, 1:]], axis=1)
    v_out = jnp.einsum('ogcthp,gk->ocgptkh', tile(v_parts), eye, precision=hi)
    v_out = v_out.reshape(n_tiles, 2 * gt * SSM_STATE, cl * V7X_LANES)
    ar, ai = pr[:, cl], pi[:, cl]
    ars, ais = [], []
    for _ in range(n_levels):
        flat_r, flat_i = ar.reshape(n_tiles, -1), ai.reshape(n_tiles, -1)
        ars.append(jnp.concatenate([flat_r, flat_r], axis=-1))
        ais.append(jnp.concatenate([-flat_i, flat_i], axis=-1))
        ar, ai = ar * ar - ai * ai, 2.0 * ar * ai
    d_row = jnp.tile(f(dd).reshape(n_tiles, V7X_LANES), (1, cl))[:, None, :]
    return (toep.astype(BF16), w_in.astype(BF16), v_out.astype(BF16),
            jnp.stack(ars, axis=1), jnp.stack(ais, axis=1), d_row)


def _s5_kernel(u_ref, t_ref, w_ref, v_ref, ar_ref, ai_ref, d_ref, o_ref, *, n_levels, period):
    cl = SSM_CHUNK
    u32 = jnp.concatenate([u_ref[pl.ds(s, period, stride=cl), :] for s in range(cl)], axis=1)
    u = u32.astype(BF16)
    x = _dot(u, w_ref[0])
    half = x.shape[1] // 2
    row = lax.broadcasted_iota(jnp.int32, x.shape, 0)
    for k in range(n_levels):
        d = 1 << k
        xs = jnp.where(row >= d, pltpu.roll(x, d, axis=0), 0.0)
        x = x + ar_ref[0, k:k + 1, :] * xs + ai_ref[0, k:k + 1, :] * pltpu.roll(xs, half, axis=1)
    xp = jnp.where(row >= 1, pltpu.roll(x, 1, axis=0), 0.0)
    xh = xp.astype(BF16)
    xl = (xp - xh.astype(F32)).astype(BF16)
    v = v_ref[0]
    y = _dot(u, t_ref[0]) + _dot(xh, v) + _dot(xl, v) + d_ref[0] * u32
    y = jax.nn.gelu(y)
    for t in range(cl):
        o_ref[pl.ds(t, period, stride=cl), :] = y[:, t * V7X_LANES:(t + 1) * V7X_LANES]


def _s5_scan(u, ops, *, bsz, slen, name):
    toep, w_in, v_out, ars, ais, d_row = ops
    n_tiles = toep.shape[0]
    period = slen // SSM_CHUNK
    n_levels = ars.shape[1]
    per_tile = lambda a: pl.BlockSpec((1,) + a.shape[1:], lambda o, b: (o, 0, 0))
    io_spec = pl.BlockSpec((slen, V7X_LANES), lambda o, b: (b, o))
    return pl.pallas_call(
        functools.partial(_s5_kernel, n_levels=n_levels, period=period),
        grid=(n_tiles, bsz),
        in_specs=[io_spec] + [per_tile(a) for a in (toep, w_in, v_out, ars, ais, d_row)],
        out_specs=io_spec,
        out_shape=jax.ShapeDtypeStruct(u.shape, F32),
        compiler_params=_cparams(2),
        name=name,
    )(u, toep, w_in, v_out, ars, ais, d_row)


def _glu_mm_kernel(g_ref, w_ref, b_ref, o_ref):
    g = g_ref[...]
    z = _dot(g.astype(w_ref.dtype), w_ref[...]) + b_ref[...]
    o_ref[...] = (g * jax.nn.sigmoid(z)).astype(o_ref.dtype)


def _glu_matmul(g, w, b, *, tm, name):
    m, k = g.shape
    n = w.shape[1]
    return pl.pallas_call(
        _glu_mm_kernel,
        grid=(m // tm,),
        in_specs=[pl.BlockSpec((tm, k), lambda i: (i, 0)),
                  pl.BlockSpec((k, n), lambda i: (0, 0)),
                  pl.BlockSpec((1, n), lambda i: (0, 0))],
        out_specs=pl.BlockSpec((tm, n), lambda i: (i, 0)),
        out_shape=jax.ShapeDtypeStruct((m, n), BF16),
        compiler_params=_cparams(1),
        name=name,
    )(g, w, b.reshape(1, n))


def _s5_mixer(u, lre, lim, log_step, bre, bim, cre, cim, dd, w_glu, b_glu, *, bsz, slen, name):
    period = slen // SSM_CHUNK
    n_levels = period.bit_length() - 1
    assert (1 << n_levels) == period and slen % SSM_CHUNK == 0
    ops = _s5_operators(lre, lim, log_step, bre, bim, cre, cim, dd, n_levels)
    g_act = _s5_scan(u.astype(F32), ops, bsz=bsz, slen=slen, name=name)
    return _glu_matmul(g_act, w_glu.astype(BF16), b_glu.astype(F32), tm=1024, name=name + "_glu")


_CONV_HALO = 32
_CONV_ROWS = 32
_CONV_COLS = 512


def _conv_ln_kernel(prev_ref, cur_ref, w_ref, b_ref, lg_ref, lb_ref, o_ref, ext_ref, y_ref, *, ts):
    n_shift = V7X_SUBLANES
    kw = w_ref.shape[0] // n_shift
    n_ch = cur_ref.shape[-1]
    span = ext_ref.shape[1]
    first = pl.program_id(1) == 0
    prev = prev_ref[0].astype(F32)
    ext_ref[0, 0:_CONV_HALO, :] = jnp.where(first, jnp.zeros_like(prev), prev)
    ext_ref[0, _CONV_HALO:, :] = cur_ref[0].astype(F32)
    for r in range(1, n_shift):
        ext_ref[r, 0:span - n_shift, :] = ext_ref[0, r:span - n_shift + r, :]
    base = _CONV_HALO - (kw - 1)
    for c0 in range(0, n_ch, _CONV_COLS):
        cols = slice(c0, c0 + _CONV_COLS)
        for r0 in range(0, ts, _CONV_ROWS):
            acc = jnp.broadcast_to(b_ref[:, cols], (_CONV_ROWS, _CONV_COLS))
            for k in range(kw):
                off = r0 + base + k
                lo = off - off % n_shift
                tap = jnp.tile(w_ref[k * n_shift:(k + 1) * n_shift, cols], (_CONV_ROWS // n_shift, 1))
                acc = acc + tap * ext_ref[off % n_shift, lo:lo + _CONV_ROWS, cols]
            y_ref[r0:r0 + _CONV_ROWS, cols] = acc
    y = y_ref[...]
    mu = jnp.mean(y, axis=-1, keepdims=True)
    yc = y - mu
    var = jnp.mean(yc * yc, axis=-1, keepdims=True)
    r = yc * lax.rsqrt(var + EPS) * lg_ref[...] + lb_ref[...]
    o_ref[0] = (r * jax.nn.sigmoid(r)).astype(o_ref.dtype)


def _conv_ln_silu(z3, w_dw, b_dw, ln_g, ln_b, *, ts, name):
    bsz, slen, n_ch = z3.shape
    kw = w_dw.shape[0]
    assert kw - 1 <= _CONV_HALO
    row = lambda a: a.astype(F32).reshape(1, n_ch)
    full = lambda r: pl.BlockSpec((r, n_ch), lambda b, i: (0, 0))
    per_halo = ts // _CONV_HALO
    w_rows = jnp.repeat(w_dw.astype(F32), V7X_SUBLANES, axis=0)
    return pl.pallas_call(
        functools.partial(_conv_ln_kernel, ts=ts),
        grid=(bsz, slen // ts),
        in_specs=[pl.BlockSpec((1, _CONV_HALO, n_ch),
                               lambda b, i: (b, jnp.maximum(i * per_halo - 1, 0), 0)),
                  pl.BlockSpec((1, ts, n_ch), lambda b, i: (b, i, 0)),
                  full(kw * V7X_SUBLANES), full(1), full(1), full(1)],
        out_specs=pl.BlockSpec((1, ts, n_ch), lambda b, i: (b, i, 0)),
        out_shape=jax.ShapeDtypeStruct((bsz, slen, n_ch), BF16),
        scratch_shapes=[pltpu.VMEM((V7X_SUBLANES, ts + _CONV_HALO, n_ch), F32),
                        pltpu.VMEM((ts, n_ch), F32)],
        compiler_params=_cparams(2),
        name=name,
    )(z3, z3, w_rows, row(b_dw), row(ln_g), row(ln_b))


def _ffn_kernel(x_ref, g_ref, wg_ref, wv_ref, cg_ref, cv_ref, bg_ref, bv_ref, wd_ref, o_ref,
                h_ref, carry_ref, *, tiles_per_seq):
    i = pl.program_id(0)
    j = pl.program_id(1)
    tm = x_ref.shape[0]
    keep = V7X_SUBLANES

    @pl.when(j == 0)
    def _():
        x = x_ref[...]
        h_ref[...] = _rms_scale(x, g_ref[...]).astype(h_ref.dtype)
        o_ref[...] = x

    @pl.when(i % tiles_per_seq == 0)
    def _():
        carry_ref[j] = jnp.zeros(carry_ref.shape[1:], carry_ref.dtype)

    h = h_ref[...]

    def conv_half(w_ref, c_ref, b_ref, slot):
        up = _dot(h, w_ref[...])
        prev = carry_ref[j, slot]
        carry_ref[j, slot] = up[tm - keep:, :]
        p1 = prev[keep - 1:keep]
        p2 = prev[keep - 2:keep - 1]
        row = lax.broadcasted_iota(jnp.int32, up.shape, 0)
        x1 = jnp.where(row == 0, p1, pltpu.roll(up, 1, axis=0))
        x2 = jnp.where(row == 0, p2, jnp.where(row == 1, p1, pltpu.roll(up, 2, axis=0)))
        return c_ref[0:1] * x2 + c_ref[1:2] * x1 + c_ref[2:3] * up + b_ref[...]

    gate = conv_half(wg_ref, cg_ref, bg_ref, 0)
    val = conv_half(wv_ref, cv_ref, bv_ref, 1)
    act = (gate * jax.nn.sigmoid(gate) * val).astype(wd_ref.dtype)
    o_ref[...] += _dot(act, wd_ref[...])


def _conv_ffn(x, g, w_up, w_dw, b_dw, w_down, *, bsz, slen, tm=1024, th=256, name):
    m, d_model = x.shape
    hidden = w_down.shape[0]
    assert w_dw.shape[0] == 3 and slen % tm == 0 and hidden % th == 0
    nj = hidden // th
    w_up = w_up.astype(BF16)
    conv_w = w_dw.astype(F32)
    conv_b = b_dw.astype(F32).reshape(1, 2 * hidden)
    up_spec = lambda off: pl.BlockSpec((d_model, th), lambda i, j: (0, j + off))
    par_spec = lambda r, off: pl.BlockSpec((r, th), lambda i, j: (0, j + off))
    return pl.pallas_call(
        functools.partial(_ffn_kernel, tiles_per_seq=slen // tm),
        grid=(m // tm, nj),
        in_specs=[pl.BlockSpec((tm, d_model), lambda i, j: (i, 0)),
                  pl.BlockSpec((1, d_model), lambda i, j: (0, 0)),
                  up_spec(0), up_spec(nj),
                  par_spec(3, 0), par_spec(3, nj), par_spec(1, 0), par_spec(1, nj),
                  pl.BlockSpec((th, d_model), lambda i, j: (j, 0))],
        out_specs=pl.BlockSpec((tm, d_model), lambda i, j: (i, 0)),
        out_shape=jax.ShapeDtypeStruct((m, d_model), F32),
        scratch_shapes=[pltpu.VMEM((tm, d_model), BF16),
                        pltpu.VMEM((nj, 2, V7X_SUBLANES, th), F32)],
        compiler_params=_cparams(2),
        name=name,
    )(x, g.astype(F32).reshape(1, d_model), w_up, w_up, conv_w, conv_w, conv_b, conv_b,
      w_down.astype(BF16))


def _rmsnorm_kernel(x_ref, g_ref, o_ref):
    o_ref[...] = _rms_scale(x_ref[...], g_ref[...])


def _rmsnorm(x, g, *, tm, name):
    m, k = x.shape
    return pl.pallas_call(
        _rmsnorm_kernel,
        grid=(m // tm,),
        in_specs=[pl.BlockSpec((tm, k), lambda i: (i, 0)),
                  pl.BlockSpec((1, k), lambda i: (0, 0))],
        out_specs=pl.BlockSpec((tm, k), lambda i: (i, 0)),
        out_shape=jax.ShapeDtypeStruct((m, k), F32),
        compiler_params=_cparams(1),
        name=name,
    )(x, g.astype(F32).reshape(1, k))


def kernel(x, rel_bias, norm_mix, norm_ffn, norm_final, ab_w_in, ab_w_out, diff_lq1, diff_lk1, diff_lq2, diff_lk2, diff_head_norm, s5_lambda_re, s5_lambda_im, s5_log_step, s5_b_re, s5_b_im, s5_c_re, s5_c_im, s5_d, s5_w_glu, s5_b_glu, conv_w_pw1, conv_w_dw, conv_b_dw, conv_ln_g, conv_ln_b, conv_w_pw2, ffn_w_up, ffn_w_dw, ffn_b_dw, ffn_w_down):
    bsz, slen, d_model = x.shape
    depth = norm_mix.shape[0]
    m = bsz * slen
    att_width = N_HEADS * HEAD_WIDTH

    xf = x.astype(F32).reshape(m, d_model)
    for layer in range(depth):
        i = layer // 2
        g_mix = norm_mix[layer].astype(F32)
        if layer % 2 == 0:
            w_in = ab_w_in[i].astype(F32)
            w_qkv = jnp.concatenate([w_in[:, :att_width] * (LOG2E * HEAD_DIM ** -0.5),
                                     w_in[:, att_width:3 * att_width]], axis=1).astype(BF16)
            qkv = _norm_matmul(xf, g_mix, w_qkv, tm=1024, tn=1024, name=f"qkv_proj_{layer}")
            u = _norm_matmul(xf, g_mix, w_in[:, 3 * att_width:].astype(BF16), out_dtype=F32,
                             tm=1024, tn=1024, name=f"u_proj_{layer}")
            lam_init = 0.8 - 0.6 * math.exp(-0.3 * layer)
            att = _diff_attention(qkv.reshape(bsz, slen, -1), rel_bias, diff_lq1[i], diff_lk1[i],
                                  diff_lq2[i], diff_lk2[i], diff_head_norm[i],
                                  lam_init=lam_init, t=512, name=f"diff_attn_{layer}")
            ssm = _s5_mixer(u, s5_lambda_re[i], s5_lambda_im[i], s5_log_step[i], s5_b_re[i],
                            s5_b_im[i], s5_c_re[i], s5_c_im[i], s5_d[i], s5_w_glu[i], s5_b_glu[i],
                            bsz=bsz, slen=slen, name=f"s5_{layer}")
            w_out = ab_w_out[i].astype(BF16)
            xf = _matmul_residual([(att.reshape(m, att_width), w_out[:att_width]),
                                   (ssm, w_out[att_width:])], xf, tm=1024, tn=1024,
                                  name=f"out_proj_{layer}")
        else:
            z = _norm_matmul(xf, g_mix, conv_w_pw1[i].astype(BF16), glu=True, tm=512, tn=1024,
                             name=f"conv_pw1_{layer}")
            c = _conv_ln_silu(z.reshape(bsz, slen, d_model), conv_w_dw[i], conv_b_dw[i],
                              conv_ln_g[i], conv_ln_b[i], ts=256, name=f"conv_dw_{layer}")
            xf = _matmul_residual([(c.reshape(m, d_model), conv_w_pw2[i].astype(BF16))], xf,
                                  tm=1024, tn=1024, name=f"conv_pw2_{layer}")
        xf = _conv_ffn(xf, norm_ffn[layer], ffn_w_up[layer], ffn_w_dw[layer], ffn_b_dw[layer],
                       ffn_w_down[layer], bsz=bsz, slen=slen, name=f"ffn_{layer}")
    out = _rmsnorm(xf, norm_final, tm=512, name="final_norm")
    return out.reshape(bsz, slen, d_model).astype(x.dtype)
```

```python
import functools
import math

import numpy as np
import jax
import jax.numpy as jnp
from jax import lax
from jax.experimental import pallas as pl
from jax.experimental.pallas import tpu as pltpu

F32 = jnp.float32
BF16 = jnp.bfloat16

N_HEADS = 8
HEAD_DIM = 64
HEAD_WIDTH = 2 * HEAD_DIM
N_BUCKETS = 32
MAX_DISTANCE = 128
SSM_GROUP = 16
SSM_STATE = 64
SSM_CHUNK = 16
EPS = 1e-6

V7X_LANES = 128
V7X_SUBLANES = 8
V7X_VMEM_LIMIT_BYTES = 56 * 1024 * 1024
MASK_VALUE = -1e30
LOG2E = math.log2(math.e)

_NT_DIMS = (((1,), (1,)), ((), ()))


def _cparams(n_axes):
    return pltpu.CompilerParams(
        dimension_semantics=("arbitrary",) * n_axes,
        vmem_limit_bytes=V7X_VMEM_LIMIT_BYTES)


def _dot(a, b):
    return jnp.dot(a, b, preferred_element_type=F32)


def _rms_scale(x, g):
    ms = jnp.mean(x * x, axis=-1, keepdims=True)
    return x * lax.rsqrt(ms + EPS) * g


def _static_lookup(table, idx):
    idx = np.asarray(idx)
    values = sorted(set(idx.ravel().tolist()))
    col = lambda b: table[b].reshape((-1,) + (1,) * idx.ndim)
    out = jnp.broadcast_to(col(values[-1]), (table.shape[1],) + idx.shape)
    for b in values[-2::-1]:
        out = jnp.where(jnp.asarray(idx == b)[None], col(b), out)
    return out


def _norm_mm_kernel(x_ref, g_ref, *rest, glu):
    if glu:
        wa_ref, wb_ref, o_ref, h_ref = rest
    else:
        w_ref, o_ref, h_ref = rest

    @pl.when(pl.program_id(1) == 0)
    def _():
        h_ref[...] = _rms_scale(x_ref[...], g_ref[...]).astype(h_ref.dtype)

    h = h_ref[...]
    if glu:
        a = _dot(h, wa_ref[...])
        gate = _dot(h, wb_ref[...])
        o_ref[...] = (a * jax.nn.sigmoid(gate)).astype(o_ref.dtype)
    else:
        o_ref[...] = _dot(h, w_ref[...]).astype(o_ref.dtype)


def _norm_matmul(x, g, w, *, glu=False, out_dtype=BF16, col_start=0, n_out=None, tm, tn, name):
    m, k = x.shape
    n = w.shape[1] // 2 if glu else (n_out or w.shape[1])
    first = col_start // tn
    in_specs = [pl.BlockSpec((tm, k), lambda i, j: (i, 0)),
                pl.BlockSpec((1, k), lambda i, j: (0, 0)),
                pl.BlockSpec((k, tn), lambda i, j: (0, j + first))]
    args = [x, g.reshape(1, k), w]
    if glu:
        in_specs.append(pl.BlockSpec((k, tn), lambda i, j: (0, j + n // tn)))
        args.append(w)
    return pl.pallas_call(
        functools.partial(_norm_mm_kernel, glu=glu),
        grid=(m // tm, n // tn),
        in_specs=in_specs,
        out_specs=pl.BlockSpec((tm, tn), lambda i, j: (i, j)),
        out_shape=jax.ShapeDtypeStruct((m, n), out_dtype),
        scratch_shapes=[pltpu.VMEM((tm, k), BF16)],
        compiler_params=_cparams(2),
        name=name,
    )(*args)


def _mm_res_norm_kernel(*refs, n_pairs):
    res_ref, g_ref, o_ref, h_ref = refs[2 * n_pairs:]
    acc = res_ref[...]
    for p in range(n_pairs):
        acc = acc + _dot(refs[2 * p][...], refs[2 * p + 1][...])
    o_ref[...] = acc
    h_ref[...] = _rms_scale(acc, g_ref[...]).astype(h_ref.dtype)


def _matmul_residual_norm(pairs, res, g, *, tm, name):
    m, n = res.shape
    in_specs, args = [], []
    for a, w, row_block in pairs:
        k = a.shape[1]
        in_specs += [pl.BlockSpec((tm, k), lambda i: (i, 0)),
                     pl.BlockSpec((k, n), functools.partial(lambda i, rb: (rb, 0), rb=row_block))]
        args += [a, w]
    in_specs += [pl.BlockSpec((tm, n), lambda i: (i, 0)), pl.BlockSpec((1, n), lambda i: (0, 0))]
    args += [res, g.astype(F32).reshape(1, n)]
    return pl.pallas_call(
        functools.partial(_mm_res_norm_kernel, n_pairs=len(pairs)),
        grid=(m // tm,),
        in_specs=in_specs,
        out_specs=[pl.BlockSpec((tm, n), lambda i: (i, 0)), pl.BlockSpec((tm, n), lambda i: (i, 0))],
        out_shape=[jax.ShapeDtypeStruct((m, n), F32), jax.ShapeDtypeStruct((m, n), BF16)],
        compiler_params=_cparams(1),
        name=name,
    )(*args)


def _t5_bucket_np(rel):
    n = np.maximum(rel, 0)
    max_exact = N_BUCKETS // 2
    nf = np.maximum(n, 1).astype(np.float32)
    large = max_exact + (np.log(nf / np.float32(max_exact))
                         / np.float32(math.log(MAX_DISTANCE / max_exact))
                         * np.float32(N_BUCKETS - max_exact)).astype(np.int32)
    large = np.minimum(large, N_BUCKETS - 1)
    return np.where(n < max_exact, n, large)


def _bias_tables(rel_bias, slen, t):
    i = np.arange(t)[:, None]
    j = np.arange(t)[None, :]
    far = _t5_bucket_np(np.arange(t + 1, max(slen, t + 2)))
    far_bucket = int(far[0])
    assert np.all(far == far_bucket)
    rb = rel_bias.astype(F32)
    rb = (rb - rb[far_bucket]) * LOG2E
    d0 = jnp.where(jnp.asarray(i >= j)[None], _static_lookup(rb, _t5_bucket_np(i - j)), MASK_VALUE)
    d1 = _static_lookup(rb, _t5_bucket_np(i - j + t))
    return jnp.stack([d0, d1], axis=1)


def _attn_kernel(q_ref, k_ref, v_ref, d_ref, lq1_ref, lk1_ref, lq2_ref, lk2_ref, hn_ref, o_ref,
                 vaug_ref, acc_ref, m_ref, *, t, lam_init):
    qi = pl.program_id(2)

    @pl.when(qi == 0)
    def _():
        vaug_ref[:, :HEAD_WIDTH] = v_ref[0]
        vaug_ref[:, HEAD_WIDTH:] = jnp.ones((vaug_ref.shape[0], HEAD_WIDTH), vaug_ref.dtype)

    q = q_ref[0]
    lane = lax.broadcasted_iota(jnp.int32, q.shape, 1)
    zero = jnp.zeros_like(q)
    q_halves = (jnp.where(lane < HEAD_DIM, q, zero), jnp.where(lane >= HEAD_DIM, q, zero))

    acc_ref[...] = jnp.zeros_like(acc_ref)
    m_ref[...] = jnp.full_like(m_ref, MASK_VALUE)

    def block(start, bias):
        kb = k_ref[0, pl.ds(start, t), :]
        vb = vaug_ref[pl.ds(start, t), :]
        for idx in range(2):
            s = lax.dot_general(q_halves[idx], kb, _NT_DIMS, preferred_element_type=F32)
            if bias is not None:
                s = s + bias
            m_old = m_ref[idx]
            m_new = jnp.maximum(m_old, jnp.max(s, axis=-1, keepdims=True))
            alpha = jnp.exp2(m_old - m_new)
            p = jnp.exp2(s - jnp.concatenate([m_new] * (t // HEAD_WIDTH), axis=1))
            pv = _dot(p.astype(vb.dtype), vb)
            acc_ref[idx] = jnp.concatenate([alpha, alpha], axis=1) * acc_ref[idx] + pv
            m_ref[idx] = m_new

    def far_body(ki, carry):
        block(pl.multiple_of(ki * t, t), None)
        return carry

    lax.fori_loop(0, qi - 1, far_body, 0)

    @pl.when(qi >= 1)
    def _():
        block(pl.multiple_of((qi - 1) * t, t), d_ref[0, 1])

    block(pl.multiple_of(qi * t, t), d_ref[0, 0])

    lam = (jnp.exp(jnp.sum(lq1_ref[...] * lk1_ref[...], keepdims=True))
           - jnp.exp(jnp.sum(lq2_ref[...] * lk2_ref[...], keepdims=True)) + lam_init)
    a1 = acc_ref[0]
    a2 = acc_ref[1]
    o = a1[:, :HEAD_WIDTH] / a1[:, HEAD_WIDTH:] - lam * (a2[:, :HEAD_WIDTH] / a2[:, HEAD_WIDTH:])
    o_ref[0] = (_rms_scale(o, hn_ref[...]) * (1.0 - lam_init)).astype(o_ref.dtype)


def _diff_attention(qkv3, rel_bias, lq1, lk1, lq2, lk2, head_norm, *, lam_init, t, name):
    bsz, slen, _ = qkv3.shape
    tiles = _bias_tables(rel_bias, slen, t)
    vec = lambda a: a.astype(F32).reshape(1, -1)
    small = lambda n: pl.BlockSpec((1, n), lambda b, h, qi: (0, 0))
    return pl.pallas_call(
        functools.partial(_attn_kernel, t=t, lam_init=lam_init),
        grid=(bsz, N_HEADS, slen // t),
        in_specs=[pl.BlockSpec((1, t, HEAD_WIDTH), lambda b, h, qi: (b, qi, h)),
                  pl.BlockSpec((1, slen, HEAD_WIDTH), lambda b, h, qi: (b, 0, N_HEADS + h)),
                  pl.BlockSpec((1, slen, HEAD_WIDTH), lambda b, h, qi: (b, 0, 2 * N_HEADS + h)),
                  pl.BlockSpec((1, 2, t, t), lambda b, h, qi: (h, 0, 0, 0)),
                  small(HEAD_DIM), small(HEAD_DIM), small(HEAD_DIM), small(HEAD_DIM),
                  small(HEAD_WIDTH)],
        out_specs=pl.BlockSpec((1, t, HEAD_WIDTH), lambda b, h, qi: (b, qi, h)),
        out_shape=jax.ShapeDtypeStruct((bsz, slen, N_HEADS * HEAD_WIDTH), BF16),
        scratch_shapes=[pltpu.VMEM((slen, 2 * HEAD_WIDTH), BF16),
                        pltpu.VMEM((2, t, 2 * HEAD_WIDTH), F32),
                        pltpu.VMEM((2, t, HEAD_WIDTH), F32)],
        compiler_params=_cparams(3),
        name=name,
    )(qkv3, qkv3, qkv3, tiles, vec(lq1), vec(lk1), vec(lq2), vec(lk2), vec(head_norm))


_GROUPS_PER_TILE = V7X_LANES // SSM_GROUP


def _s5_operators(lre, lim, log_step, bre, bim, cre, cim, dd, n_levels):
    f = lambda a: a.astype(F32)
    lre, lim, bre, bim, cre, cim = map(f, (lre, lim, bre, bim, cre, cim))
    n_groups = lre.shape[0]
    cl, gt = SSM_CHUNK, _GROUPS_PER_TILE
    n_tiles = n_groups // gt
    hi = lax.Precision.HIGHEST
    dt = jnp.exp(f(log_step))[:, None]
    zr, zi = lre * dt, lim * dt
    steps = jnp.arange(cl + 1, dtype=F32)[None, :, None]
    mag = jnp.exp(zr[:, None, :] * steps)
    ang = zi[:, None, :] * steps
    pr, pi = mag * jnp.cos(ang), mag * jnp.sin(ang)
    nr = jnp.expm1(zr) * jnp.cos(zi) - 2.0 * jnp.sin(0.5 * zi) ** 2
    ni = jnp.exp(zr) * jnp.sin(zi)
    den = lre * lre + lim * lim
    fr, fi = (nr * lre + ni * lim) / den, (ni * lre - nr * lim) / den
    bbr = fr[..., None] * bre - fi[..., None] * bim
    bbi = fr[..., None] * bim + fi[..., None] * bre
    cpr = cre[:, None] * pr[:, :, None, :] - cim[:, None] * pi[:, :, None, :]
    cpi = cre[:, None] * pi[:, :, None, :] + cim[:, None] * pr[:, :, None, :]
    kern = (jnp.einsum('gjhp,gpi->gjhi', cpr[:, :cl], bbr, precision=hi)
            - jnp.einsum('gjhp,gpi->gjhi', cpi[:, :cl], bbi, precision=hi))
    eye = jnp.eye(gt, dtype=F32)
    tile = lambda a: a.reshape((n_tiles, gt) + a.shape[1:])
    lag_blocks = jnp.einsum('ogjhi,gk->ojgikh', tile(kern), eye, precision=hi)
    lag_blocks = lag_blocks.reshape(n_tiles, cl, V7X_LANES, V7X_LANES)
    rev = np.arange(cl - 1, -1, -1)
    prr, pir = pr[:, rev][:, :, None, :], pi[:, rev][:, :, None, :]
    bbr_t, bbi_t = bbr.transpose(0, 2, 1)[:, None], bbi.transpose(0, 2, 1)[:, None]
    w_parts = jnp.stack([prr * bbr_t - pir * bbi_t, prr * bbi_t + pir * bbr_t], axis=3)
    w_rows = tile(w_parts).transpose(0, 2, 1, 3, 4, 5).reshape(n_tiles, cl * V7X_LANES, 2 * SSM_STATE)
    v_parts = jnp.stack([cpr[:, 1:], -cpi[:, 1:]], axis=3)
    vt_rows = tile(v_parts).transpose(0, 2, 1, 3, 4, 5).reshape(n_tiles, cl * V7X_LANES, 2 * SSM_STATE)
    ar, ai = pr[:, cl], pi[:, cl]
    ars, ais = [], []
    for _ in range(n_levels):
        ars.append(jnp.concatenate([ar, ar], axis=-1).reshape(n_tiles, -1))
        ais.append(jnp.concatenate([-ai, ai], axis=-1).reshape(n_tiles, -1))
        ar, ai = ar * ar - ai * ai, 2.0 * ar * ai
    d_row = jnp.tile(f(dd).reshape(n_tiles, V7X_LANES), (1, cl))[:, None, :]
    return (lag_blocks.astype(BF16), w_rows.astype(BF16), vt_rows.astype(BF16),
            jnp.stack(ars, axis=1), jnp.stack(ais, axis=1), d_row)


def _s5_kernel(u_ref, bd_ref, wr_ref, vr_ref, ar_ref, ai_ref, d_ref, o_ref,
               t_ref, w_ref, vt_ref, *, n_levels, period):
    cl, gt, lanes = SSM_CHUNK, _GROUPS_PER_TILE, V7X_LANES

    @pl.when(pl.program_id(1) == 0)
    def _():
        zeros = jnp.zeros((lanes, lanes), t_ref.dtype)
        for s in range(cl):
            for t in range(cl):
                t_ref[s * lanes:(s + 1) * lanes, t * lanes:(t + 1) * lanes] = (
                    bd_ref[0, t - s] if t >= s else zeros)
        w_rows = wr_ref[0].astype(F32)
        v_rows = vr_ref[0].astype(F32)
        group = (lax.broadcasted_iota(jnp.int32, w_rows.shape, 0) // SSM_GROUP) % gt
        for g in range(gt):
            cols = slice(g * lanes, (g + 1) * lanes)
            w_ref[:, cols] = jnp.where(group == g, w_rows, 0.0).astype(w_ref.dtype)
            vt_ref[:, cols] = jnp.where(group == g, v_rows, 0.0).astype(vt_ref.dtype)

    u32 = jnp.concatenate([u_ref[pl.ds(s, period, stride=cl), :] for s in range(cl)], axis=1)
    u = u32.astype(BF16)
    x = _dot(u, w_ref[...])
    row = lax.broadcasted_iota(jnp.int32, x.shape, 0)

    def swap_re_im(a):
        return jnp.concatenate(
            [pltpu.roll(a[:, g * lanes:(g + 1) * lanes], SSM_STATE, axis=1) for g in range(gt)], axis=1)

    for k in range(n_levels):
        d = 1 << k
        xs = jnp.where(row >= d, pltpu.roll(x, d, axis=0), 0.0)
        x = x + ar_ref[0, k:k + 1, :] * xs + ai_ref[0, k:k + 1, :] * swap_re_im(xs)
    xp = jnp.where(row >= 1, pltpu.roll(x, 1, axis=0), 0.0)
    xh = xp.astype(BF16)
    xl = (xp - xh.astype(F32)).astype(BF16)
    vt = vt_ref[...]
    nt_dot = lambda a, b: lax.dot_general(a, b, _NT_DIMS, preferred_element_type=F32)
    y = _dot(u, t_ref[...]) + nt_dot(xh, vt) + nt_dot(xl, vt) + d_ref[0] * u32
    y = jax.nn.gelu(y)
    for t in range(cl):
        o_ref[pl.ds(t, period, stride=cl), :] = y[:, t * lanes:(t + 1) * lanes]


def _s5_scan(u, ops, *, bsz, slen, name):
    lag_blocks, w_rows, vt_rows, ars, ais, d_row = ops
    n_tiles = lag_blocks.shape[0]
    period = slen // SSM_CHUNK
    n_levels = ars.shape[1]
    width = SSM_CHUNK * V7X_LANES
    state = _GROUPS_PER_TILE * 2 * SSM_STATE
    per_tile = lambda a: pl.BlockSpec((1,) + a.shape[1:], lambda o, b: (o,) + (0,) * (a.ndim - 1))
    io_spec = pl.BlockSpec((slen, V7X_LANES), lambda o, b: (b, o))
    return pl.pallas_call(
        functools.partial(_s5_kernel, n_levels=n_levels, period=period),
        grid=(n_tiles, bsz),
        in_specs=[io_spec] + [per_tile(a) for a in (lag_blocks, w_rows, vt_rows, ars, ais, d_row)],
        out_specs=io_spec,
        out_shape=jax.ShapeDtypeStruct(u.shape, F32),
        scratch_shapes=[pltpu.VMEM((width, width), BF16),
                        pltpu.VMEM((width, state), BF16),
                        pltpu.VMEM((width, state), BF16)],
        compiler_params=_cparams(2),
        name=name,
    )(u, lag_blocks, w_rows, vt_rows, ars, ais, d_row)


def _glu_mm_kernel(g_ref, w_ref, b_ref, o_ref):
    g = g_ref[...]
    z = _dot(g.astype(w_ref.dtype), w_ref[...]) + b_ref[...]
    o_ref[...] = (g * jax.nn.sigmoid(z)).astype(o_ref.dtype)


def _glu_matmul(g, w, b, *, tm, name):
    m, k = g.shape
    n = w.shape[1]
    return pl.pallas_call(
        _glu_mm_kernel,
        grid=(m // tm,),
        in_specs=[pl.BlockSpec((tm, k), lambda i: (i, 0)),
                  pl.BlockSpec((k, n), lambda i: (0, 0)),
                  pl.BlockSpec((1, n), lambda i: (0, 0))],
        out_specs=pl.BlockSpec((tm, n), lambda i: (i, 0)),
        out_shape=jax.ShapeDtypeStruct((m, n), BF16),
        compiler_params=_cparams(1),
        name=name,
    )(g, w, b.reshape(1, n))


def _s5_mixer(u, lre, lim, log_step, bre, bim, cre, cim, dd, w_glu, b_glu, *, bsz, slen, name):
    period = slen // SSM_CHUNK
    n_levels = period.bit_length() - 1
    assert (1 << n_levels) == period and slen % SSM_CHUNK == 0
    ops = _s5_operators(lre, lim, log_step, bre, bim, cre, cim, dd, n_levels)
    g_act = _s5_scan(u.astype(F32), ops, bsz=bsz, slen=slen, name=name)
    return _glu_matmul(g_act, w_glu.astype(BF16), b_glu.astype(F32), tm=1024, name=name + "_glu")


_CONV_HALO = 32
_CONV_ROWS = 32
_CONV_COLS = 512


def _conv_ln_kernel(prev_ref, cur_ref, w_ref, b_ref, lg_ref, lb_ref, o_ref, ext_ref, y_ref, *, ts):
    n_shift = V7X_SUBLANES
    kw = w_ref.shape[0] // n_shift
    n_ch = cur_ref.shape[-1]
    span = ext_ref.shape[1]
    first = pl.program_id(1) == 0
    prev = prev_ref[0].astype(F32)
    ext_ref[0, 0:_CONV_HALO, :] = jnp.where(first, jnp.zeros_like(prev), prev)
    ext_ref[0, _CONV_HALO:, :] = cur_ref[0].astype(F32)
    for r in range(1, n_shift):
        ext_ref[r, 0:span - n_shift, :] = ext_ref[0, r:span - n_shift + r, :]
    base = _CONV_HALO - (kw - 1)
    for c0 in range(0, n_ch, _CONV_COLS):
        cols = slice(c0, c0 + _CONV_COLS)
        for r0 in range(0, ts, _CONV_ROWS):
            acc = jnp.broadcast_to(b_ref[:, cols], (_CONV_ROWS, _CONV_COLS))
            for k in range(kw):
                off = r0 + base + k
                lo = off - off % n_shift
                tap = jnp.tile(w_ref[k * n_shift:(k + 1) * n_shift, cols], (_CONV_ROWS // n_shift, 1))
                acc = acc + tap * ext_ref[off % n_shift, lo:lo + _CONV_ROWS, cols]
            y_ref[r0:r0 + _CONV_ROWS, cols] = acc
    y = y_ref[...]
    mu = jnp.mean(y, axis=-1, keepdims=True)
    yc = y - mu
    var = jnp.mean(yc * yc, axis=-1, keepdims=True)
    r = yc * lax.rsqrt(var + EPS) * lg_ref[...] + lb_ref[...]
    o_ref[0] = (r * jax.nn.sigmoid(r)).astype(o_ref.dtype)


def _conv_ln_silu(z3, w_dw, b_dw, ln_g, ln_b, *, ts, name):
    bsz, slen, n_ch = z3.shape
    kw = w_dw.shape[0]
    assert kw - 1 <= _CONV_HALO
    row = lambda a: a.astype(F32).reshape(1, n_ch)
    full = lambda r: pl.BlockSpec((r, n_ch), lambda b, i: (0, 0))
    per_halo = ts // _CONV_HALO
    w_rows = jnp.repeat(w_dw.astype(F32), V7X_SUBLANES, axis=0)
    return pl.pallas_call(
        functools.partial(_conv_ln_kernel, ts=ts),
        grid=(bsz, slen // ts),
        in_specs=[pl.BlockSpec((1, _CONV_HALO, n_ch),
                               lambda b, i: (b, jnp.maximum(i * per_halo - 1, 0), 0)),
                  pl.BlockSpec((1, ts, n_ch), lambda b, i: (b, i, 0)),
                  full(kw * V7X_SUBLANES), full(1), full(1), full(1)],
        out_specs=pl.BlockSpec((1, ts, n_ch), lambda b, i: (b, i, 0)),
        out_shape=jax.ShapeDtypeStruct((bsz, slen, n_ch), BF16),
        scratch_shapes=[pltpu.VMEM((V7X_SUBLANES, ts + _CONV_HALO, n_ch), F32),
                        pltpu.VMEM((ts, n_ch), F32)],
        compiler_params=_cparams(2),
        name=name,
    )(z3, z3, w_rows, row(b_dw), row(ln_g), row(ln_b))


def _ffn_kernel(h_ref, wg_ref, wv_ref, cg_ref, cv_ref, bg_ref, bv_ref, wd_ref, x_ref, o_ref,
                act_ref, carry_ref, *, n_up, tiles_per_seq):
    i = pl.program_id(0)
    j = pl.program_id(1)
    tm = h_ref.shape[0]
    th = wg_ref.shape[1]
    keep = V7X_SUBLANES

    @pl.when(j < n_up)
    def _():
        @pl.when(i % tiles_per_seq == 0)
        def _():
            carry_ref[j] = jnp.zeros(carry_ref.shape[1:], carry_ref.dtype)

        h = h_ref[...]

        def conv_half(w_ref, c_ref, b_ref, slot):
            up = _dot(h, w_ref[...])
            prev = carry_ref[j, slot]
            carry_ref[j, slot] = up[tm - keep:, :]
            p1 = prev[keep - 1:keep]
            p2 = prev[keep - 2:keep - 1]
            row = lax.broadcasted_iota(jnp.int32, up.shape, 0)
            x1 = jnp.where(row == 0, p1, pltpu.roll(up, 1, axis=0))
            x2 = jnp.where(row == 0, p2, jnp.where(row == 1, p1, pltpu.roll(up, 2, axis=0)))
            return c_ref[0:1] * x2 + c_ref[1:2] * x1 + c_ref[2:3] * up + b_ref[...]

        gate = conv_half(wg_ref, cg_ref, bg_ref, 0)
        val = conv_half(wv_ref, cv_ref, bv_ref, 1)
        act = gate * jax.nn.sigmoid(gate) * val
        act_ref[:, pl.ds(pl.multiple_of(j * th, th), th)] = act.astype(act_ref.dtype)

    @pl.when(j >= n_up)
    def _():
        o_ref[...] = x_ref[...] + _dot(act_ref[...], wd_ref[...])


def _conv_ffn(x, h, w_up, w_dw, b_dw, w_down, *, bsz, slen, tm=1024, th=256, tn=512, name):
    m, d_model = x.shape
    hidden = w_down.shape[0]
    assert w_dw.shape[0] == 3 and slen % tm == 0 and hidden % th == 0 and d_model % tn == 0
    n_up = hidden // th
    n_dn = d_model // tn
    w_up = w_up.astype(BF16)
    conv_w = w_dw.astype(F32)
    conv_b = b_dw.astype(F32).reshape(1, 2 * hidden)
    up_block = lambda j: jnp.minimum(j, n_up - 1)
    dn_block = lambda j: jnp.maximum(j - n_up, 0)
    up_spec = lambda off: pl.BlockSpec((d_model, th), lambda i, j: (0, up_block(j) + off))
    par_spec = lambda r, off: pl.BlockSpec((r, th), lambda i, j: (0, up_block(j) + off))
    io_spec = pl.BlockSpec((tm, tn), lambda i, j: (i, dn_block(j)))
    return pl.pallas_call(
        functools.partial(_ffn_kernel, n_up=n_up, tiles_per_seq=slen // tm),
        grid=(m // tm, n_up + n_dn),
        in_specs=[pl.BlockSpec((tm, d_model), lambda i, j: (i, 0)),
                  up_spec(0), up_spec(n_up),
                  par_spec(3, 0), par_spec(3, n_up), par_spec(1, 0), par_spec(1, n_up),
                  pl.BlockSpec((hidden, tn), lambda i, j: (0, dn_block(j))),
                  io_spec],
        out_specs=io_spec,
        out_shape=jax.ShapeDtypeStruct((m, d_model), F32),
        scratch_shapes=[pltpu.VMEM((tm, hidden), BF16),
                        pltpu.VMEM((n_up, 2, V7X_SUBLANES, th), F32)],
        compiler_params=_cparams(2),
        name=name,
    )(h, w_up, w_up, conv_w, conv_w, conv_b, conv_b, w_down.astype(BF16), x)


def _rmsnorm_kernel(x_ref, g_ref, o_ref):
    o_ref[...] = _rms_scale(x_ref[...], g_ref[...])


def _rmsnorm(x, g, *, tm, name):
    m, k = x.shape
    return pl.pallas_call(
        _rmsnorm_kernel,
        grid=(m // tm,),
        in_specs=[pl.BlockSpec((tm, k), lambda i: (i, 0)),
                  pl.BlockSpec((1, k), lambda i: (0, 0))],
        out_specs=pl.BlockSpec((tm, k), lambda i: (i, 0)),
        out_shape=jax.ShapeDtypeStruct((m, k), F32),
        compiler_params=_cparams(1),
        name=name,
    )(x, g.astype(F32).reshape(1, k))


def kernel(x, rel_bias, norm_mix, norm_ffn, norm_final, ab_w_in, ab_w_out, diff_lq1, diff_lk1, diff_lq2, diff_lk2, diff_head_norm, s5_lambda_re, s5_lambda_im, s5_log_step, s5_b_re, s5_b_im, s5_c_re, s5_c_im, s5_d, s5_w_glu, s5_b_glu, conv_w_pw1, conv_w_dw, conv_b_dw, conv_ln_g, conv_ln_b, conv_w_pw2, ffn_w_up, ffn_w_dw, ffn_b_dw, ffn_w_down):
    bsz, slen, d_model = x.shape
    depth = norm_mix.shape[0]
    m = bsz * slen
    att_width = N_HEADS * HEAD_WIDTH

    xf = x.astype(F32).reshape(m, d_model)
    for layer in range(depth):
        i = layer // 2
        g_mix = norm_mix[layer].astype(F32)
        if layer % 2 == 0:
            col_scale = jnp.where(jnp.arange(ab_w_in.shape[-1]) < att_width,
                                  LOG2E * HEAD_DIM ** -0.5, 1.0).astype(F32)
            w_in = (ab_w_in[i].astype(F32) * col_scale).astype(BF16)
            qkv = _norm_matmul(xf, g_mix, w_in, n_out=3 * att_width, tm=1024, tn=1024,
                               name=f"qkv_proj_{layer}")
            u = _norm_matmul(xf, g_mix, w_in, col_start=3 * att_width, n_out=att_width,
                             out_dtype=F32, tm=1024, tn=1024, name=f"u_proj_{layer}")
            lam_init = 0.8 - 0.6 * math.exp(-0.3 * layer)
            att = _diff_attention(qkv.reshape(bsz, slen, -1), rel_bias, diff_lq1[i], diff_lk1[i],
                                  diff_lq2[i], diff_lk2[i], diff_head_norm[i],
                                  lam_init=lam_init, t=512, name=f"diff_attn_{layer}")
            ssm = _s5_mixer(u, s5_lambda_re[i], s5_lambda_im[i], s5_log_step[i], s5_b_re[i],
                            s5_b_im[i], s5_c_re[i], s5_c_im[i], s5_d[i], s5_w_glu[i], s5_b_glu[i],
                            bsz=bsz, slen=slen, name=f"s5_{layer}")
            w_out = ab_w_out[i].astype(BF16)
            xf, h_ffn = _matmul_residual_norm(
                [(att.reshape(m, att_width), w_out, 0), (ssm, w_out, 1)], xf, norm_ffn[layer],
                tm=512, name=f"out_proj_{layer}")
        else:
            z = _norm_matmul(xf, g_mix, conv_w_pw1[i].astype(BF16), glu=True, tm=512, tn=1024,
                             name=f"conv_pw1_{layer}")
            c = _conv_ln_silu(z.reshape(bsz, slen, d_model), conv_w_dw[i], conv_b_dw[i],
                              conv_ln_g[i], conv_ln_b[i], ts=256, name=f"conv_dw_{layer}")
            xf, h_ffn = _matmul_residual_norm(
                [(c.reshape(m, d_model), conv_w_pw2[i].astype(BF16), 0)], xf, norm_ffn[layer],
                tm=512, name=f"conv_pw2_{layer}")
        xf = _conv_ffn(xf, h_ffn, ffn_w_up[layer], ffn_w_dw[layer], ffn_b_dw[layer],
                       ffn_w_down[layer], bsz=bsz, slen=slen, name=f"ffn_{layer}")
    out = _rmsnorm(xf, norm_final, tm=512, name="final_norm")
    return out.reshape(bsz, slen, d_model).astype(x.dtype)
```

```python
import functools
import math

import numpy as np
import jax
import jax.numpy as jnp
from jax import lax
from jax.experimental import pallas as pl
from jax.experimental.pallas import tpu as pltpu

F32 = jnp.float32
BF16 = jnp.bfloat16

N_HEADS = 8
HEAD_DIM = 64
HEAD_WIDTH = 2 * HEAD_DIM
N_BUCKETS = 32
MAX_DISTANCE = 128
SSM_GROUP = 16
SSM_STATE = 64
SSM_CHUNK = 16
EPS = 1e-6

V7X_LANES = 128
V7X_SUBLANES = 8
V7X_VMEM_LIMIT_BYTES = 56 * 1024 * 1024
MASK_VALUE = -1e30
LOG2E = math.log2(math.e)

_NT_DIMS = (((1,), (1,)), ((), ()))


def _cparams(n_axes):
    return pltpu.CompilerParams(
        dimension_semantics=("arbitrary",) * n_axes,
        vmem_limit_bytes=V7X_VMEM_LIMIT_BYTES)


def _dot(a, b):
    return jnp.dot(a, b, preferred_element_type=F32)


def _rms_scale(x, g):
    ms = jnp.mean(x * x, axis=-1, keepdims=True)
    return x * lax.rsqrt(ms + EPS) * g


def _by_parity(n, fn):
    @pl.when(n % 2 == 0)
    def _():
        fn(0)

    @pl.when(n % 2 == 1)
    def _():
        fn(1)


def _static_lookup(table, idx):
    idx = np.asarray(idx)
    values = sorted(set(idx.ravel().tolist()))
    col = lambda b: table[b].reshape((-1,) + (1,) * idx.ndim)
    out = jnp.broadcast_to(col(values[-1]), (table.shape[1],) + idx.shape)
    for b in values[-2::-1]:
        out = jnp.where(jnp.asarray(idx == b)[None], col(b), out)
    return out


def _norm_mm_kernel(x_ref, g_ref, *rest, glu):
    if glu:
        wa_ref, wb_ref, o_ref, h_ref = rest
    else:
        w_ref, o_ref, h_ref = rest

    @pl.when(pl.program_id(1) == 0)
    def _():
        h_ref[...] = _rms_scale(x_ref[...], g_ref[...]).astype(h_ref.dtype)

    h = h_ref[...]
    if glu:
        a = _dot(h, wa_ref[...])
        gate = _dot(h, wb_ref[...])
        o_ref[...] = (a * jax.nn.sigmoid(gate)).astype(o_ref.dtype)
    else:
        o_ref[...] = _dot(h, w_ref[...]).astype(o_ref.dtype)


def _norm_matmul(x, g, w, *, glu=False, out_dtype=BF16, col_start=0, n_out=None, tm, tn, name):
    m, k = x.shape
    n = w.shape[1] // 2 if glu else (n_out or w.shape[1])
    first = col_start // tn
    in_specs = [pl.BlockSpec((tm, k), lambda i, j: (i, 0)),
                pl.BlockSpec((1, k), lambda i, j: (0, 0)),
                pl.BlockSpec((k, tn), lambda i, j: (0, j + first))]
    args = [x, g.reshape(1, k), w]
    if glu:
        in_specs.append(pl.BlockSpec((k, tn), lambda i, j: (0, j + n // tn)))
        args.append(w)
    return pl.pallas_call(
        functools.partial(_norm_mm_kernel, glu=glu),
        grid=(m // tm, n // tn),
        in_specs=in_specs,
        out_specs=pl.BlockSpec((tm, tn), lambda i, j: (i, j)),
        out_shape=jax.ShapeDtypeStruct((m, n), out_dtype),
        scratch_shapes=[pltpu.VMEM((tm, k), BF16)],
        compiler_params=_cparams(2),
        name=name,
    )(*args)


def _mm_res_norm_kernel(*refs, n_pairs):
    res_ref, g_ref, o_ref, h_ref = refs[2 * n_pairs:]
    acc = res_ref[...]
    for p in range(n_pairs):
        acc = acc + _dot(refs[2 * p][...], refs[2 * p + 1][...])
    o_ref[...] = acc
    h_ref[...] = _rms_scale(acc, g_ref[...]).astype(h_ref.dtype)


def _matmul_residual_norm(pairs, res, g, *, tm, name):
    m, n = res.shape
    in_specs, args = [], []
    for a, w, row_block in pairs:
        k = a.shape[1]
        in_specs += [pl.BlockSpec((tm, k), lambda i: (i, 0)),
                     pl.BlockSpec((k, n), functools.partial(lambda i, rb: (rb, 0), rb=row_block))]
        args += [a, w]
    in_specs += [pl.BlockSpec((tm, n), lambda i: (i, 0)), pl.BlockSpec((1, n), lambda i: (0, 0))]
    args += [res, g.astype(F32).reshape(1, n)]
    return pl.pallas_call(
        functools.partial(_mm_res_norm_kernel, n_pairs=len(pairs)),
        grid=(m // tm,),
        in_specs=in_specs,
        out_specs=[pl.BlockSpec((tm, n), lambda i: (i, 0)), pl.BlockSpec((tm, n), lambda i: (i, 0))],
        out_shape=[jax.ShapeDtypeStruct((m, n), F32), jax.ShapeDtypeStruct((m, n), BF16)],
        compiler_params=_cparams(1),
        name=name,
    )(*args)


def _t5_bucket_np(rel):
    n = np.maximum(rel, 0)
    max_exact = N_BUCKETS // 2
    nf = np.maximum(n, 1).astype(np.float32)
    large = max_exact + (np.log(nf / np.float32(max_exact))
                         / np.float32(math.log(MAX_DISTANCE / max_exact))
                         * np.float32(N_BUCKETS - max_exact)).astype(np.int32)
    large = np.minimum(large, N_BUCKETS - 1)
    return np.where(n < max_exact, n, large)


def _bias_tables(rel_bias, slen, t):
    i = np.arange(t)[:, None]
    j = np.arange(t)[None, :]
    far = _t5_bucket_np(np.arange(t + 1, max(slen, t + 2)))
    far_bucket = int(far[0])
    assert np.all(far == far_bucket)
    rb = rel_bias.astype(F32)
    rb = (rb - rb[far_bucket]) * LOG2E
    d0 = jnp.where(jnp.asarray(i >= j)[None], _static_lookup(rb, _t5_bucket_np(i - j)), MASK_VALUE)
    d1 = _static_lookup(rb, _t5_bucket_np(i - j + t))
    return jnp.stack([d0, d1], axis=1)


def _attn_kernel(q_ref, k_ref, v_ref, d_ref, lq1_ref, lk1_ref, lq2_ref, lk2_ref, hn_ref, o_ref,
                 vaug_ref, qs_ref, s_ref, acc_ref, m_ref, *, t, lam_init):
    qi = pl.program_id(2)

    @pl.when(qi == 0)
    def _():
        vaug_ref[:, :HEAD_WIDTH] = v_ref[0]
        vaug_ref[:, HEAD_WIDTH:] = jnp.ones((vaug_ref.shape[0], HEAD_WIDTH), vaug_ref.dtype)

    q = q_ref[0]
    lane = lax.broadcasted_iota(jnp.int32, q.shape, 1)
    zero = jnp.zeros_like(q)
    qs_ref[:t] = jnp.where(lane < HEAD_DIM, q, zero)
    qs_ref[t:] = jnp.where(lane >= HEAD_DIM, q, zero)

    acc_ref[...] = jnp.zeros_like(acc_ref)
    m_ref[...] = jnp.full_like(m_ref, MASK_VALUE)

    def scores(blk, slot):
        kb = k_ref[0, pl.ds(pl.multiple_of(blk * t, t), t), :]
        s_ref[slot] = lax.dot_general(qs_ref[...], kb, _NT_DIMS, preferred_element_type=F32)

    def update(blk, slot, bias):
        s = s_ref[slot]
        if bias is not None:
            s = jnp.concatenate([s[:t] + bias, s[t:] + bias], axis=0)
        vb = vaug_ref[pl.ds(pl.multiple_of(blk * t, t), t), :]
        m_old = m_ref[...]
        m_new = jnp.maximum(m_old, jnp.max(s, axis=-1, keepdims=True))
        alpha = jnp.exp2(m_old - m_new)
        p = jnp.exp2(s - jnp.concatenate([m_new] * (t // HEAD_WIDTH), axis=1))
        pv = _dot(p.astype(vb.dtype), vb)
        acc_ref[...] = jnp.concatenate([alpha, alpha], axis=1) * acc_ref[...] + pv
        m_ref[...] = m_new

    scores(0, 0)

    def far_body(blk, carry):
        _by_parity(blk, lambda cur: (scores(blk + 1, 1 - cur), update(blk, cur, None)))
        return carry

    lax.fori_loop(0, qi - 1, far_body, 0)

    @pl.when(qi >= 1)
    def _():
        _by_parity(qi - 1, lambda cur: (scores(qi, 1 - cur), update(qi - 1, cur, d_ref[0, 1])))

    _by_parity(qi, lambda cur: update(qi, cur, d_ref[0, 0]))

    lam = (jnp.exp(jnp.sum(lq1_ref[...] * lk1_ref[...], keepdims=True))
           - jnp.exp(jnp.sum(lq2_ref[...] * lk2_ref[...], keepdims=True)) + lam_init)
    a1 = acc_ref[:t]
    a2 = acc_ref[t:]
    o = a1[:, :HEAD_WIDTH] / a1[:, HEAD_WIDTH:] - lam * (a2[:, :HEAD_WIDTH] / a2[:, HEAD_WIDTH:])
    o_ref[0] = (_rms_scale(o, hn_ref[...]) * (1.0 - lam_init)).astype(o_ref.dtype)


def _diff_attention(qkv3, rel_bias, lq1, lk1, lq2, lk2, head_norm, *, lam_init, t, name):
    bsz, slen, _ = qkv3.shape
    tiles = _bias_tables(rel_bias, slen, t)
    vec = lambda a: a.astype(F32).reshape(1, -1)
    small = lambda n: pl.BlockSpec((1, n), lambda b, h, qi: (0, 0))
    return pl.pallas_call(
        functools.partial(_attn_kernel, t=t, lam_init=lam_init),
        grid=(bsz, N_HEADS, slen // t),
        in_specs=[pl.BlockSpec((1, t, HEAD_WIDTH), lambda b, h, qi: (b, qi, h)),
                  pl.BlockSpec((1, slen, HEAD_WIDTH), lambda b, h, qi: (b, 0, N_HEADS + h)),
                  pl.BlockSpec((1, slen, HEAD_WIDTH), lambda b, h, qi: (b, 0, 2 * N_HEADS + h)),
                  pl.BlockSpec((1, 2, t, t), lambda b, h, qi: (h, 0, 0, 0)),
                  small(HEAD_DIM), small(HEAD_DIM), small(HEAD_DIM), small(HEAD_DIM),
                  small(HEAD_WIDTH)],
        out_specs=pl.BlockSpec((1, t, HEAD_WIDTH), lambda b, h, qi: (b, qi, h)),
        out_shape=jax.ShapeDtypeStruct((bsz, slen, N_HEADS * HEAD_WIDTH), BF16),
        scratch_shapes=[pltpu.VMEM((slen, 2 * HEAD_WIDTH), BF16),
                        pltpu.VMEM((2 * t, HEAD_WIDTH), BF16),
                        pltpu.VMEM((2, 2 * t, t), F32),
                        pltpu.VMEM((2 * t, 2 * HEAD_WIDTH), F32),
                        pltpu.VMEM((2 * t, HEAD_WIDTH), F32)],
        compiler_params=_cparams(3),
        name=name,
    )(qkv3, qkv3, qkv3, tiles, vec(lq1), vec(lk1), vec(lq2), vec(lk2), vec(head_norm))


_GROUPS_PER_TILE = V7X_LANES // SSM_GROUP


def _s5_operators(lre, lim, log_step, bre, bim, cre, cim, dd, n_levels):
    f = lambda a: a.astype(F32)
    lre, lim, bre, bim, cre, cim = map(f, (lre, lim, bre, bim, cre, cim))
    n_groups = lre.shape[0]
    cl, gt = SSM_CHUNK, _GROUPS_PER_TILE
    n_tiles = n_groups // gt
    hi = lax.Precision.HIGHEST
    dt = jnp.exp(f(log_step))[:, None]
    zr, zi = lre * dt, lim * dt
    steps = jnp.arange(cl + 1, dtype=F32)[None, :, None]
    mag = jnp.exp(zr[:, None, :] * steps)
    ang = zi[:, None, :] * steps
    pr, pi = mag * jnp.cos(ang), mag * jnp.sin(ang)
    nr = jnp.expm1(zr) * jnp.cos(zi) - 2.0 * jnp.sin(0.5 * zi) ** 2
    ni = jnp.exp(zr) * jnp.sin(zi)
    den = lre * lre + lim * lim
    fr, fi = (nr * lre + ni * lim) / den, (ni * lre - nr * lim) / den
    bbr = fr[..., None] * bre - fi[..., None] * bim
    bbi = fr[..., None] * bim + fi[..., None] * bre
    cpr = cre[:, None] * pr[:, :, None, :] - cim[:, None] * pi[:, :, None, :]
    cpi = cre[:, None] * pi[:, :, None, :] + cim[:, None] * pr[:, :, None, :]
    kern = (jnp.einsum('gjhp,gpi->gjhi', cpr[:, :cl], bbr, precision=hi)
            - jnp.einsum('gjhp,gpi->gjhi', cpi[:, :cl], bbi, precision=hi))
    eye = jnp.eye(gt, dtype=F32)
    tile = lambda a: a.reshape((n_tiles, gt) + a.shape[1:])
    lag_blocks = jnp.einsum('ogjhi,gk->ojgikh', tile(kern), eye, precision=hi)
    lag_blocks = lag_blocks.reshape(n_tiles, cl, V7X_LANES, V7X_LANES)
    rev = np.arange(cl - 1, -1, -1)
    prr, pir = pr[:, rev][:, :, None, :], pi[:, rev][:, :, None, :]
    bbr_t, bbi_t = bbr.transpose(0, 2, 1)[:, None], bbi.transpose(0, 2, 1)[:, None]
    w_parts = jnp.stack([prr * bbr_t - pir * bbi_t, prr * bbi_t + pir * bbr_t], axis=3)
    w_rows = tile(w_parts).transpose(0, 2, 1, 3, 4, 5).reshape(n_tiles, cl * V7X_LANES, 2 * SSM_STATE)
    v_parts = jnp.stack([cpr[:, 1:], -cpi[:, 1:]], axis=3)
    vt_rows = tile(v_parts).transpose(0, 2, 1, 3, 4, 5).reshape(n_tiles, cl * V7X_LANES, 2 * SSM_STATE)
    ar, ai = pr[:, cl], pi[:, cl]
    ars, ais = [], []
    for _ in range(n_levels):
        ars.append(jnp.concatenate([ar, ar], axis=-1).reshape(n_tiles, -1))
        ais.append(jnp.concatenate([-ai, ai], axis=-1).reshape(n_tiles, -1))
        ar, ai = ar * ar - ai * ai, 2.0 * ar * ai
    d_row = jnp.tile(f(dd).reshape(n_tiles, V7X_LANES), (1, cl))[:, None, :]
    return (lag_blocks.astype(BF16), w_rows.astype(BF16), vt_rows.astype(BF16),
            jnp.stack(ars, axis=1), jnp.stack(ais, axis=1), d_row)


def _s5_kernel(u_ref, bd_ref, wr_ref, vr_ref, ar_ref, ai_ref, d_ref, o_ref,
               t_ref, w_ref, vt_ref, *, n_levels, period):
    cl, gt, lanes = SSM_CHUNK, _GROUPS_PER_TILE, V7X_LANES

    @pl.when(pl.program_id(1) == 0)
    def _():
        zeros = jnp.zeros((lanes, lanes), t_ref.dtype)
        for s in range(cl):
            for t in range(cl):
                t_ref[s * lanes:(s + 1) * lanes, t * lanes:(t + 1) * lanes] = (
                    bd_ref[0, t - s] if t >= s else zeros)
        w_rows = wr_ref[0].astype(F32)
        v_rows = vr_ref[0].astype(F32)
        group = (lax.broadcasted_iota(jnp.int32, w_rows.shape, 0) // SSM_GROUP) % gt
        for g in range(gt):
            cols = slice(g * lanes, (g + 1) * lanes)
            w_ref[:, cols] = jnp.where(group == g, w_rows, 0.0).astype(w_ref.dtype)
            vt_ref[:, cols] = jnp.where(group == g, v_rows, 0.0).astype(vt_ref.dtype)

    u32 = jnp.concatenate([u_ref[pl.ds(s, period, stride=cl), :] for s in range(cl)], axis=1)
    u = u32.astype(BF16)
    x = _dot(u, w_ref[...])
    row = lax.broadcasted_iota(jnp.int32, x.shape, 0)

    def swap_re_im(a):
        return jnp.concatenate(
            [pltpu.roll(a[:, g * lanes:(g + 1) * lanes], SSM_STATE, axis=1) for g in range(gt)], axis=1)

    for k in range(n_levels):
        d = 1 << k
        xs = jnp.where(row >= d, pltpu.roll(x, d, axis=0), 0.0)
        x = x + ar_ref[0, k:k + 1, :] * xs + ai_ref[0, k:k + 1, :] * swap_re_im(xs)
    xp = jnp.where(row >= 1, pltpu.roll(x, 1, axis=0), 0.0)
    xh = xp.astype(BF16)
    xl = (xp - xh.astype(F32)).astype(BF16)
    vt = vt_ref[...]
    nt_dot = lambda a, b: lax.dot_general(a, b, _NT_DIMS, preferred_element_type=F32)
    y = _dot(u, t_ref[...]) + nt_dot(xh, vt) + nt_dot(xl, vt) + d_ref[0] * u32
    y = jax.nn.gelu(y)
    for t in range(cl):
        o_ref[pl.ds(t, period, stride=cl), :] = y[:, t * lanes:(t + 1) * lanes]


def _s5_scan(u, ops, *, bsz, slen, name):
    lag_blocks, w_rows, vt_rows, ars, ais, d_row = ops
    n_tiles = lag_blocks.shape[0]
    period = slen // SSM_CHUNK
    n_levels = ars.shape[1]
    width = SSM_CHUNK * V7X_LANES
    state = _GROUPS_PER_TILE * 2 * SSM_STATE
    per_tile = lambda a: pl.BlockSpec((1,) + a.shape[1:], lambda o, b: (o,) + (0,) * (a.ndim - 1))
    io_spec = pl.BlockSpec((slen, V7X_LANES), lambda o, b: (b, o))
    return pl.pallas_call(
        functools.partial(_s5_kernel, n_levels=n_levels, period=period),
        grid=(n_tiles, bsz),
        in_specs=[io_spec] + [per_tile(a) for a in (lag_blocks, w_rows, vt_rows, ars, ais, d_row)],
        out_specs=io_spec,
        out_shape=jax.ShapeDtypeStruct(u.shape, F32),
        scratch_shapes=[pltpu.VMEM((width, width), BF16),
                        pltpu.VMEM((width, state), BF16),
                        pltpu.VMEM((width, state), BF16)],
        compiler_params=_cparams(2),
        name=name,
    )(u, lag_blocks, w_rows, vt_rows, ars, ais, d_row)


def _glu_mm_kernel(g_ref, w_ref, b_ref, o_ref):
    g = g_ref[...]
    z = _dot(g.astype(w_ref.dtype), w_ref[...]) + b_ref[...]
    o_ref[...] = (g * jax.nn.sigmoid(z)).astype(o_ref.dtype)


def _glu_matmul(g, w, b, *, tm, name):
    m, k = g.shape
    n = w.shape[1]
    return pl.pallas_call(
        _glu_mm_kernel,
        grid=(m // tm,),
        in_specs=[pl.BlockSpec((tm, k), lambda i: (i, 0)),
                  pl.BlockSpec((k, n), lambda i: (0, 0)),
                  pl.BlockSpec((1, n), lambda i: (0, 0))],
        out_specs=pl.BlockSpec((tm, n), lambda i: (i, 0)),
        out_shape=jax.ShapeDtypeStruct((m, n), BF16),
        compiler_params=_cparams(1),
        name=name,
    )(g, w, b.reshape(1, n))


def _s5_mixer(u, lre, lim, log_step, bre, bim, cre, cim, dd, w_glu, b_glu, *, bsz, slen, name):
    period = slen // SSM_CHUNK
    n_levels = period.bit_length() - 1
    assert (1 << n_levels) == period and slen % SSM_CHUNK == 0
    ops = _s5_operators(lre, lim, log_step, bre, bim, cre, cim, dd, n_levels)
    g_act = _s5_scan(u.astype(F32), ops, bsz=bsz, slen=slen, name=name)
    return _glu_matmul(g_act, w_glu.astype(BF16), b_glu.astype(F32), tm=1024, name=name + "_glu")


_CONV_HALO = 32
_CONV_ROWS = 32
_CONV_COLS = 512


def _conv_ln_kernel(prev_ref, cur_ref, w_ref, b_ref, lg_ref, lb_ref, o_ref, ext_ref, y_ref, *, ts):
    n_shift = V7X_SUBLANES
    kw = w_ref.shape[0] // n_shift
    n_ch = cur_ref.shape[-1]
    span = ext_ref.shape[1]
    first = pl.program_id(1) == 0
    prev = prev_ref[0].astype(F32)
    ext_ref[0, 0:_CONV_HALO, :] = jnp.where(first, jnp.zeros_like(prev), prev)
    ext_ref[0, _CONV_HALO:, :] = cur_ref[0].astype(F32)
    for r in range(1, n_shift):
        ext_ref[r, 0:span - n_shift, :] = ext_ref[0, r:span - n_shift + r, :]
    base = _CONV_HALO - (kw - 1)
    for c0 in range(0, n_ch, _CONV_COLS):
        cols = slice(c0, c0 + _CONV_COLS)
        for r0 in range(0, ts, _CONV_ROWS):
            acc = jnp.broadcast_to(b_ref[:, cols], (_CONV_ROWS, _CONV_COLS))
            for k in range(kw):
                off = r0 + base + k
                lo = off - off % n_shift
                tap = jnp.tile(w_ref[k * n_shift:(k + 1) * n_shift, cols], (_CONV_ROWS // n_shift, 1))
                acc = acc + tap * ext_ref[off % n_shift, lo:lo + _CONV_ROWS, cols]
            y_ref[r0:r0 + _CONV_ROWS, cols] = acc
    y = y_ref[...]
    mu = jnp.mean(y, axis=-1, keepdims=True)
    yc = y - mu
    var = jnp.mean(yc * yc, axis=-1, keepdims=True)
    r = yc * lax.rsqrt(var + EPS) * lg_ref[...] + lb_ref[...]
    o_ref[0] = (r * jax.nn.sigmoid(r)).astype(o_ref.dtype)


def _conv_ln_silu(z3, w_dw, b_dw, ln_g, ln_b, *, ts, name):
    bsz, slen, n_ch = z3.shape
    kw = w_dw.shape[0]
    assert kw - 1 <= _CONV_HALO
    row = lambda a: a.astype(F32).reshape(1, n_ch)
    full = lambda r: pl.BlockSpec((r, n_ch), lambda b, i: (0, 0))
    per_halo = ts // _CONV_HALO
    w_rows = jnp.repeat(w_dw.astype(F32), V7X_SUBLANES, axis=0)
    return pl.pallas_call(
        functools.partial(_conv_ln_kernel, ts=ts),
        grid=(bsz, slen // ts),
        in_specs=[pl.BlockSpec((1, _CONV_HALO, n_ch),
                               lambda b, i: (b, jnp.maximum(i * per_halo - 1, 0), 0)),
                  pl.BlockSpec((1, ts, n_ch), lambda b, i: (b, i, 0)),
                  full(kw * V7X_SUBLANES), full(1), full(1), full(1)],
        out_specs=pl.BlockSpec((1, ts, n_ch), lambda b, i: (b, i, 0)),
        out_shape=jax.ShapeDtypeStruct((bsz, slen, n_ch), BF16),
        scratch_shapes=[pltpu.VMEM((V7X_SUBLANES, ts + _CONV_HALO, n_ch), F32),
                        pltpu.VMEM((ts, n_ch), F32)],
        compiler_params=_cparams(2),
        name=name,
    )(z3, z3, w_rows, row(b_dw), row(ln_g), row(ln_b))


def _ffn_kernel(h_ref, wg_ref, wv_ref, cg_ref, cv_ref, bg_ref, bv_ref, wd_ref, x_ref, o_ref,
                act_ref, carry_ref, *, n_up, tiles_per_seq):
    i = pl.program_id(0)
    j = pl.program_id(1)
    tm = h_ref.shape[0]
    th = wg_ref.shape[1]
    keep = V7X_SUBLANES

    @pl.when(j < n_up)
    def _():
        @pl.when(i % tiles_per_seq == 0)
        def _():
            carry_ref[j] = jnp.zeros(carry_ref.shape[1:], carry_ref.dtype)

        h = h_ref[...]

        def conv_half(w_ref, c_ref, b_ref, slot):
            up = _dot(h, w_ref[...])
            prev = carry_ref[j, slot]
            carry_ref[j, slot] = up[tm - keep:, :]
            p1 = prev[keep - 1:keep]
            p2 = prev[keep - 2:keep - 1]
            row = lax.broadcasted_iota(jnp.int32, up.shape, 0)
            x1 = jnp.where(row == 0, p1, pltpu.roll(up, 1, axis=0))
            x2 = jnp.where(row == 0, p2, jnp.where(row == 1, p1, pltpu.roll(up, 2, axis=0)))
            return c_ref[0:1] * x2 + c_ref[1:2] * x1 + c_ref[2:3] * up + b_ref[...]

        gate = conv_half(wg_ref, cg_ref, bg_ref, 0)
        val = conv_half(wv_ref, cv_ref, bv_ref, 1)
        act = gate * jax.nn.sigmoid(gate) * val
        act_ref[:, pl.ds(pl.multiple_of(j * th, th), th)] = act.astype(act_ref.dtype)

    @pl.when(j >= n_up)
    def _():
        o_ref[...] = x_ref[...] + _dot(act_ref[...], wd_ref[...])


def _conv_ffn(x, h, w_up, w_dw, b_dw, w_down, *, bsz, slen, tm=1024, th=512, tn=512, name):
    m, d_model = x.shape
    hidden = w_down.shape[0]
    assert w_dw.shape[0] == 3 and slen % tm == 0 and hidden % th == 0 and d_model % tn == 0
    n_up = hidden // th
    n_dn = d_model // tn
    w_up = w_up.astype(BF16)
    conv_w = w_dw.astype(F32)
    conv_b = b_dw.astype(F32).reshape(1, 2 * hidden)
    up_block = lambda j: jnp.minimum(j, n_up - 1)
    dn_block = lambda j: jnp.maximum(j - n_up, 0)
    up_spec = lambda off: pl.BlockSpec((d_model, th), lambda i, j: (0, up_block(j) + off))
    par_spec = lambda r, off: pl.BlockSpec((r, th), lambda i, j: (0, up_block(j) + off))
    io_spec = pl.BlockSpec((tm, tn), lambda i, j: (i, dn_block(j)))
    return pl.pallas_call(
        functools.partial(_ffn_kernel, n_up=n_up, tiles_per_seq=slen // tm),
        grid=(m // tm, n_up + n_dn),
        in_specs=[pl.BlockSpec((tm, d_model), lambda i, j: (i, 0)),
                  up_spec(0), up_spec(n_up),
                  par_spec(3, 0), par_spec(3, n_up), par_spec(1, 0), par_spec(1, n_up),
                  pl.BlockSpec((hidden, tn), lambda i, j: (0, dn_block(j))),
                  io_spec],
        out_specs=io_spec,
        out_shape=jax.ShapeDtypeStruct((m, d_model), F32),
        scratch_shapes=[pltpu.VMEM((tm, hidden), BF16),
                        pltpu.VMEM((n_up, 2, V7X_SUBLANES, th), F32)],
        compiler_params=_cparams(2),
        name=name,
    )(h, w_up, w_up, conv_w, conv_w, conv_b, conv_b, w_down.astype(BF16), x)


def _rmsnorm_kernel(x_ref, g_ref, o_ref):
    o_ref[...] = _rms_scale(x_ref[...], g_ref[...])


def _rmsnorm(x, g, *, tm, name):
    m, k = x.shape
    return pl.pallas_call(
        _rmsnorm_kernel,
        grid=(m // tm,),
        in_specs=[pl.BlockSpec((tm, k), lambda i: (i, 0)),
                  pl.BlockSpec((1, k), lambda i: (0, 0))],
        out_specs=pl.BlockSpec((tm, k), lambda i: (i, 0)),
        out_shape=jax.ShapeDtypeStruct((m, k), F32),
        compiler_params=_cparams(1),
        name=name,
    )(x, g.astype(F32).reshape(1, k))


def kernel(x, rel_bias, norm_mix, norm_ffn, norm_final, ab_w_in, ab_w_out, diff_lq1, diff_lk1, diff_lq2, diff_lk2, diff_head_norm, s5_lambda_re, s5_lambda_im, s5_log_step, s5_b_re, s5_b_im, s5_c_re, s5_c_im, s5_d, s5_w_glu, s5_b_glu, conv_w_pw1, conv_w_dw, conv_b_dw, conv_ln_g, conv_ln_b, conv_w_pw2, ffn_w_up, ffn_w_dw, ffn_b_dw, ffn_w_down):
    bsz, slen, d_model = x.shape
    depth = norm_mix.shape[0]
    m = bsz * slen
    att_width = N_HEADS * HEAD_WIDTH

    xf = x.astype(F32).reshape(m, d_model)
    for layer in range(depth):
        i = layer // 2
        g_mix = norm_mix[layer].astype(F32)
        if layer % 2 == 0:
            col_scale = jnp.where(jnp.arange(ab_w_in.shape[-1]) < att_width,
                                  LOG2E * HEAD_DIM ** -0.5, 1.0).astype(F32)
            w_in = (ab_w_in[i].astype(F32) * col_scale).astype(BF16)
            qkv = _norm_matmul(xf, g_mix, w_in, n_out=3 * att_width, tm=1024, tn=1024,
                               name=f"qkv_proj_{layer}")
            u = _norm_matmul(xf, g_mix, w_in, col_start=3 * att_width, n_out=att_width,
                             out_dtype=F32, tm=1024, tn=1024, name=f"u_proj_{layer}")
            lam_init = 0.8 - 0.6 * math.exp(-0.3 * layer)
            att = _diff_attention(qkv.reshape(bsz, slen, -1), rel_bias, diff_lq1[i], diff_lk1[i],
                                  diff_lq2[i], diff_lk2[i], diff_head_norm[i],
                                  lam_init=lam_init, t=512, name=f"diff_attn_{layer}")
            ssm = _s5_mixer(u, s5_lambda_re[i], s5_lambda_im[i], s5_log_step[i], s5_b_re[i],
                            s5_b_im[i], s5_c_re[i], s5_c_im[i], s5_d[i], s5_w_glu[i], s5_b_glu[i],
                            bsz=bsz, slen=slen, name=f"s5_{layer}")
            w_out = ab_w_out[i].astype(BF16)
            xf, h_ffn = _matmul_residual_norm(
                [(att.reshape(m, att_width), w_out, 0), (ssm, w_out, 1)], xf, norm_ffn[layer],
                tm=512, name=f"out_proj_{layer}")
        else:
            z = _norm_matmul(xf, g_mix, conv_w_pw1[i].astype(BF16), glu=True, tm=512, tn=1024,
                             name=f"conv_pw1_{layer}")
            c = _conv_ln_silu(z.reshape(bsz, slen, d_model), conv_w_dw[i], conv_b_dw[i],
                              conv_ln_g[i], conv_ln_b[i], ts=256, name=f"conv_dw_{layer}")
            xf, h_ffn = _matmul_residual_norm(
                [(c.reshape(m, d_model), conv_w_pw2[i].astype(BF16), 0)], xf, norm_ffn[layer],
                tm=512, name=f"conv_pw2_{layer}")
        xf = _conv_ffn(xf, h_ffn, ffn_w_up[layer], ffn_w_dw[layer], ffn_b_dw[layer],
                       ffn_w_down[layer], bsz=bsz, slen=slen, name=f"ffn_{layer}")
    out = _rmsnorm(xf, norm_final, tm=512, name="final_norm")
    return out.reshape(bsz, slen, d_model).astype(x.dtype)
```

```python
import functools
import math

import numpy as np
import jax
import jax.numpy as jnp
from jax import lax
from jax.experimental import pallas as pl
from jax.experimental.pallas import tpu as pltpu

F32 = jnp.float32
BF16 = jnp.bfloat16

N_HEADS = 8
HEAD_DIM = 64
HEAD_WIDTH = 2 * HEAD_DIM
N_BUCKETS = 32
MAX_DISTANCE = 128
SSM_GROUP = 16
SSM_STATE = 64
SSM_CHUNK = 16
EPS = 1e-6

V7X_LANES = 128
V7X_SUBLANES = 8
V7X_VMEM_LIMIT_BYTES = 56 * 1024 * 1024
MASK_VALUE = -1e30
LOG2E = math.log2(math.e)

_NT_DIMS = (((1,), (1,)), ((), ()))


def _cparams(n_axes):
    return pltpu.CompilerParams(
        dimension_semantics=("arbitrary",) * n_axes,
        vmem_limit_bytes=V7X_VMEM_LIMIT_BYTES)


def _dot(a, b):
    return jnp.dot(a, b, preferred_element_type=F32)


def _rms_scale(x, g):
    ms = jnp.mean(x * x, axis=-1, keepdims=True)
    return x * lax.rsqrt(ms + EPS) * g


def _by_parity(n, fn):
    @pl.when(n % 2 == 0)
    def _():
        fn(0)

    @pl.when(n % 2 == 1)
    def _():
        fn(1)


def _static_lookup(table, idx):
    idx = np.asarray(idx)
    values = sorted(set(idx.ravel().tolist()))
    col = lambda b: table[b].reshape((-1,) + (1,) * idx.ndim)
    out = jnp.broadcast_to(col(values[-1]), (table.shape[1],) + idx.shape)
    for b in values[-2::-1]:
        out = jnp.where(jnp.asarray(idx == b)[None], col(b), out)
    return out


def _layer_spec(layer, block, index_map):
    return pl.BlockSpec((None,) + block, lambda *idx: (layer,) + index_map(*idx))


def _norm_glu_kernel(x_ref, g_ref, wa_ref, wb_ref, o_ref, h_ref):
    @pl.when(pl.program_id(1) == 0)
    def _():
        h_ref[...] = _rms_scale(x_ref[...], g_ref[...]).astype(h_ref.dtype)

    h = h_ref[...]
    a = _dot(h, wa_ref[...])
    gate = _dot(h, wb_ref[...])
    o_ref[...] = (a * jax.nn.sigmoid(gate)).astype(o_ref.dtype)


def _norm_glu_matmul(x, g, w_all, layer, *, tm, tn, name):
    m, k = x.shape
    n = w_all.shape[2] // 2
    return pl.pallas_call(
        _norm_glu_kernel,
        grid=(m // tm, n // tn),
        in_specs=[pl.BlockSpec((tm, k), lambda i, j: (i, 0)),
                  pl.BlockSpec((1, k), lambda i, j: (0, 0)),
                  _layer_spec(layer, (k, tn), lambda i, j: (0, j)),
                  _layer_spec(layer, (k, tn), lambda i, j: (0, j + n // tn))],
        out_specs=pl.BlockSpec((tm, tn), lambda i, j: (i, j)),
        out_shape=jax.ShapeDtypeStruct((m, n), BF16),
        scratch_shapes=[pltpu.VMEM((tm, k), BF16)],
        compiler_params=_cparams(2),
        name=name,
    )(x, g.reshape(1, k), w_all, w_all)


def _norm_split_kernel(x_ref, g_ref, w_ref, a_ref, b_ref, h_ref, *, n_first):
    j = pl.program_id(1)

    @pl.when(j == 0)
    def _():
        h_ref[...] = _rms_scale(x_ref[...], g_ref[...]).astype(h_ref.dtype)

    r = _dot(h_ref[...], w_ref[...])

    @pl.when(j < n_first)
    def _():
        a_ref[...] = r.astype(a_ref.dtype)

    @pl.when(j >= n_first)
    def _():
        b_ref[...] = r.astype(b_ref.dtype)


def _norm_matmul_split(x, g, w_all, layer, *, n_a, tm, tn, name):
    m, k = x.shape
    n = w_all.shape[2]
    n_first = n_a // tn
    return pl.pallas_call(
        functools.partial(_norm_split_kernel, n_first=n_first),
        grid=(m // tm, n // tn),
        in_specs=[pl.BlockSpec((tm, k), lambda i, j: (i, 0)),
                  pl.BlockSpec((1, k), lambda i, j: (0, 0)),
                  _layer_spec(layer, (k, tn), lambda i, j: (0, j))],
        out_specs=[pl.BlockSpec((tm, tn), lambda i, j: (i, jnp.minimum(j, n_first - 1))),
                   pl.BlockSpec((tm, tn), lambda i, j: (i, jnp.maximum(j - n_first, 0)))],
        out_shape=[jax.ShapeDtypeStruct((m, n_a), BF16), jax.ShapeDtypeStruct((m, n - n_a), F32)],
        scratch_shapes=[pltpu.VMEM((tm, k), BF16)],
        compiler_params=_cparams(2),
        name=name,
    )(x, g.reshape(1, k), w_all)


def _mm_res_norm_kernel(*refs, n_pairs):
    res_ref, g_ref, o_ref, h_ref = refs[2 * n_pairs:]
    acc = res_ref[...]
    for p in range(n_pairs):
        acc = acc + _dot(refs[2 * p][...], refs[2 * p + 1][...])
    o_ref[...] = acc
    h_ref[...] = _rms_scale(acc, g_ref[...]).astype(h_ref.dtype)


def _matmul_residual_norm(pairs, w_all, layer, res, g, *, tm, name):
    m, n = res.shape
    in_specs, args = [], []
    for a, row_block in pairs:
        k = a.shape[1]
        in_specs += [pl.BlockSpec((tm, k), lambda i: (i, 0)),
                     _layer_spec(layer, (k, n), functools.partial(lambda i, rb: (rb, 0), rb=row_block))]
        args += [a, w_all]
    in_specs += [pl.BlockSpec((tm, n), lambda i: (i, 0)), pl.BlockSpec((1, n), lambda i: (0, 0))]
    args += [res, g.astype(F32).reshape(1, n)]
    return pl.pallas_call(
        functools.partial(_mm_res_norm_kernel, n_pairs=len(pairs)),
        grid=(m // tm,),
        in_specs=in_specs,
        out_specs=[pl.BlockSpec((tm, n), lambda i: (i, 0)), pl.BlockSpec((tm, n), lambda i: (i, 0))],
        out_shape=[jax.ShapeDtypeStruct((m, n), F32), jax.ShapeDtypeStruct((m, n), BF16)],
        compiler_params=_cparams(1),
        name=name,
    )(*args)


def _t5_bucket_np(rel):
    n = np.maximum(rel, 0)
    max_exact = N_BUCKETS // 2
    nf = np.maximum(n, 1).astype(np.float32)
    large = max_exact + (np.log(nf / np.float32(max_exact))
                         / np.float32(math.log(MAX_DISTANCE / max_exact))
                         * np.float32(N_BUCKETS - max_exact)).astype(np.int32)
    large = np.minimum(large, N_BUCKETS - 1)
    return np.where(n < max_exact, n, large)


def _bias_tables(rel_bias, slen, t):
    i = np.arange(t)[:, None]
    j = np.arange(t)[None, :]
    far = _t5_bucket_np(np.arange(t + 1, max(slen, t + 2)))
    far_bucket = int(far[0])
    assert np.all(far == far_bucket)
    rb = rel_bias.astype(F32)
    rb = (rb - rb[far_bucket]) * LOG2E
    d0 = jnp.where(jnp.asarray(i >= j)[None], _static_lookup(rb, _t5_bucket_np(i - j)), MASK_VALUE)
    d1 = _static_lookup(rb, _t5_bucket_np(i - j + t))
    return jnp.stack([d0, d1], axis=1)


def _attn_kernel(q_ref, k_ref, v_ref, d_ref, lq1_ref, lk1_ref, lq2_ref, lk2_ref, hn_ref, o_ref,
                 vaug_ref, qs_ref, s_ref, acc_ref, m_ref, *, t, lam_init):
    qi = pl.program_id(2)

    @pl.when(qi == 0)
    def _():
        vaug_ref[:, :HEAD_WIDTH] = v_ref[0]
        vaug_ref[:, HEAD_WIDTH:] = jnp.ones((vaug_ref.shape[0], HEAD_WIDTH), vaug_ref.dtype)

    q = q_ref[0]
    lane = lax.broadcasted_iota(jnp.int32, q.shape, 1)
    zero = jnp.zeros_like(q)
    qs_ref[:t] = jnp.where(lane < HEAD_DIM, q, zero)
    qs_ref[t:] = jnp.where(lane >= HEAD_DIM, q, zero)

    acc_ref[...] = jnp.zeros_like(acc_ref)
    m_ref[...] = jnp.full_like(m_ref, MASK_VALUE)

    def scores(blk, slot):
        kb = k_ref[0, pl.ds(pl.multiple_of(blk * t, t), t), :]
        s_ref[slot] = lax.dot_general(qs_ref[...], kb, _NT_DIMS, preferred_element_type=F32)

    def update(blk, slot, bias):
        s = s_ref[slot]
        if bias is not None:
            s = jnp.concatenate([s[:t] + bias, s[t:] + bias], axis=0)
        vb = vaug_ref[pl.ds(pl.multiple_of(blk * t, t), t), :]
        m_old = m_ref[...]
        m_new = jnp.maximum(m_old, jnp.max(s, axis=-1, keepdims=True))
        alpha = jnp.exp2(m_old - m_new)
        p = jnp.exp2(s - jnp.concatenate([m_new] * (t // HEAD_WIDTH), axis=1))
        pv = _dot(p.astype(vb.dtype), vb)
        acc_ref[...] = jnp.concatenate([alpha, alpha], axis=1) * acc_ref[...] + pv
        m_ref[...] = m_new

    scores(0, 0)

    def far_body(blk, carry):
        _by_parity(blk, lambda cur: (scores(blk + 1, 1 - cur), update(blk, cur, None)))
        return carry

    lax.fori_loop(0, qi - 1, far_body, 0)

    @pl.when(qi >= 1)
    def _():
        _by_parity(qi - 1, lambda cur: (scores(qi, 1 - cur), update(qi - 1, cur, d_ref[0, 1])))

    _by_parity(qi, lambda cur: update(qi, cur, d_ref[0, 0]))

    lam = (jnp.exp(jnp.sum(lq1_ref[...] * lk1_ref[...], keepdims=True))
           - jnp.exp(jnp.sum(lq2_ref[...] * lk2_ref[...], keepdims=True)) + lam_init)
    a1 = acc_ref[:t]
    a2 = acc_ref[t:]
    o = a1[:, :HEAD_WIDTH] / a1[:, HEAD_WIDTH:] - lam * (a2[:, :HEAD_WIDTH] / a2[:, HEAD_WIDTH:])
    o_ref[0] = (_rms_scale(o, hn_ref[...]) * (1.0 - lam_init)).astype(o_ref.dtype)


def _diff_attention(qkv3, rel_bias, lq1, lk1, lq2, lk2, head_norm, *, lam_init, t, name):
    bsz, slen, _ = qkv3.shape
    tiles = _bias_tables(rel_bias, slen, t)
    vec = lambda a: a.astype(F32).reshape(1, -1)
    small = lambda n: pl.BlockSpec((1, n), lambda b, h, qi: (0, 0))
    return pl.pallas_call(
        functools.partial(_attn_kernel, t=t, lam_init=lam_init),
        grid=(bsz, N_HEADS, slen // t),
        in_specs=[pl.BlockSpec((1, t, HEAD_WIDTH), lambda b, h, qi: (b, qi, h)),
                  pl.BlockSpec((1, slen, HEAD_WIDTH), lambda b, h, qi: (b, 0, N_HEADS + h)),
                  pl.BlockSpec((1, slen, HEAD_WIDTH), lambda b, h, qi: (b, 0, 2 * N_HEADS + h)),
                  pl.BlockSpec((1, 2, t, t), lambda b, h, qi: (h, 0, 0, 0)),
                  small(HEAD_DIM), small(HEAD_DIM), small(HEAD_DIM), small(HEAD_DIM),
                  small(HEAD_WIDTH)],
        out_specs=pl.BlockSpec((1, t, HEAD_WIDTH), lambda b, h, qi: (b, qi, h)),
        out_shape=jax.ShapeDtypeStruct((bsz, slen, N_HEADS * HEAD_WIDTH), BF16),
        scratch_shapes=[pltpu.VMEM((slen, 2 * HEAD_WIDTH), BF16),
                        pltpu.VMEM((2 * t, HEAD_WIDTH), BF16),
                        pltpu.VMEM((2, 2 * t, t), F32),
                        pltpu.VMEM((2 * t, 2 * HEAD_WIDTH), F32),
                        pltpu.VMEM((2 * t, HEAD_WIDTH), F32)],
        compiler_params=_cparams(3),
        name=name,
    )(qkv3, qkv3, qkv3, tiles, vec(lq1), vec(lk1), vec(lq2), vec(lk2), vec(head_norm))


_GROUPS_PER_TILE = V7X_LANES // SSM_GROUP


def _s5_operators(lre, lim, log_step, bre, bim, cre, cim, dd, n_levels):
    f = lambda a: a.astype(F32)
    lre, lim, bre, bim, cre, cim = map(f, (lre, lim, bre, bim, cre, cim))
    n_groups = lre.shape[0]
    cl, gt = SSM_CHUNK, _GROUPS_PER_TILE
    n_tiles = n_groups // gt
    hi = lax.Precision.HIGHEST
    dt = jnp.exp(f(log_step))[:, None]
    zr, zi = lre * dt, lim * dt
    steps = jnp.arange(cl + 1, dtype=F32)[None, :, None]
    mag = jnp.exp(zr[:, None, :] * steps)
    ang = zi[:, None, :] * steps
    pr, pi = mag * jnp.cos(ang), mag * jnp.sin(ang)
    nr = jnp.expm1(zr) * jnp.cos(zi) - 2.0 * jnp.sin(0.5 * zi) ** 2
    ni = jnp.exp(zr) * jnp.sin(zi)
    den = lre * lre + lim * lim
    fr, fi = (nr * lre + ni * lim) / den, (ni * lre - nr * lim) / den
    bbr = fr[..., None] * bre - fi[..., None] * bim
    bbi = fr[..., None] * bim + fi[..., None] * bre
    cpr = cre[:, None] * pr[:, :, None, :] - cim[:, None] * pi[:, :, None, :]
    cpi = cre[:, None] * pi[:, :, None, :] + cim[:, None] * pr[:, :, None, :]
    kern = (jnp.einsum('gjhp,gpi->gjhi', cpr[:, :cl], bbr, precision=hi)
            - jnp.einsum('gjhp,gpi->gjhi', cpi[:, :cl], bbi, precision=hi))
    eye = jnp.eye(gt, dtype=F32)
    tile = lambda a: a.reshape((n_tiles, gt) + a.shape[1:])
    lag_blocks = jnp.einsum('ogjhi,gk->ojgikh', tile(kern), eye, precision=hi)
    lag_blocks = lag_blocks.reshape(n_tiles, cl, V7X_LANES, V7X_LANES)
    pr_t, pi_t = tile(pr).transpose(0, 2, 1, 3), tile(pi).transpose(0, 2, 1, 3)
    rev = np.arange(cl - 1, -1, -1)
    prr, pir = pr_t[:, rev][:, :, :, None, :], pi_t[:, rev][:, :, :, None, :]
    bbr_t = tile(bbr.transpose(0, 2, 1))[:, None]
    bbi_t = tile(bbi.transpose(0, 2, 1))[:, None]
    w_rows = jnp.concatenate([prr * bbr_t - pir * bbi_t, prr * bbi_t + pir * bbr_t], axis=-1)
    w_rows = w_rows.reshape(n_tiles, cl * V7X_LANES, 2 * SSM_STATE)
    ptr, pti = pr_t[:, 1:][:, :, :, None, :], pi_t[:, 1:][:, :, :, None, :]
    cre_t, cim_t = tile(cre)[:, None], tile(cim)[:, None]
    vt_rows = jnp.concatenate([cre_t * ptr - cim_t * pti, -(cre_t * pti + cim_t * ptr)], axis=-1)
    vt_rows = vt_rows.reshape(n_tiles, cl * V7X_LANES, 2 * SSM_STATE)
    ar, ai = pr[:, cl], pi[:, cl]
    ars, ais = [], []
    for _ in range(n_levels):
        ars.append(jnp.concatenate([ar, ar], axis=-1).reshape(n_tiles, -1))
        ais.append(jnp.concatenate([-ai, ai], axis=-1).reshape(n_tiles, -1))
        ar, ai = ar * ar - ai * ai, 2.0 * ar * ai
    d_row = jnp.tile(f(dd).reshape(n_tiles, V7X_LANES), (1, cl))[:, None, :]
    return (lag_blocks.astype(BF16), w_rows.astype(BF16), vt_rows.astype(BF16),
            jnp.stack(ars, axis=1), jnp.stack(ais, axis=1), d_row)


def _s5_kernel(u_ref, bd_ref, wr_ref, vr_ref, ar_ref, ai_ref, d_ref, o_ref,
               t_ref, w_ref, vt_ref, *, n_levels, period):
    cl, gt, lanes = SSM_CHUNK, _GROUPS_PER_TILE, V7X_LANES

    @pl.when(pl.program_id(1) == 0)
    def _():
        zeros = jnp.zeros((lanes, lanes), t_ref.dtype)
        for s in range(cl):
            for t in range(cl):
                t_ref[s * lanes:(s + 1) * lanes, t * lanes:(t + 1) * lanes] = (
                    bd_ref[0, t - s] if t >= s else zeros)
        w_rows = wr_ref[0].astype(F32)
        v_rows = vr_ref[0].astype(F32)
        group = (lax.broadcasted_iota(jnp.int32, w_rows.shape, 0) // SSM_GROUP) % gt
        for g in range(gt):
            cols = slice(g * lanes, (g + 1) * lanes)
            w_ref[:, cols] = jnp.where(group == g, w_rows, 0.0).astype(w_ref.dtype)
            vt_ref[:, cols] = jnp.where(group == g, v_rows, 0.0).astype(vt_ref.dtype)

    u32 = jnp.concatenate([u_ref[pl.ds(s, period, stride=cl), :] for s in range(cl)], axis=1)
    u = u32.astype(BF16)
    x = _dot(u, w_ref[...])
    row = lax.broadcasted_iota(jnp.int32, x.shape, 0)

    def swap_re_im(a):
        return jnp.concatenate(
            [pltpu.roll(a[:, g * lanes:(g + 1) * lanes], SSM_STATE, axis=1) for g in range(gt)], axis=1)

    for k in range(n_levels):
        d = 1 << k
        xs = jnp.where(row >= d, pltpu.roll(x, d, axis=0), 0.0)
        x = x + ar_ref[0, k:k + 1, :] * xs + ai_ref[0, k:k + 1, :] * swap_re_im(xs)
    xp = jnp.where(row >= 1, pltpu.roll(x, 1, axis=0), 0.0)
    xh = xp.astype(BF16)
    xl = (xp - xh.astype(F32)).astype(BF16)
    vt = vt_ref[...]
    nt_dot = lambda a, b: lax.dot_general(a, b, _NT_DIMS, preferred_element_type=F32)
    y = _dot(u, t_ref[...]) + nt_dot(xh, vt) + nt_dot(xl, vt) + d_ref[0] * u32
    y = jax.nn.gelu(y)
    for t in range(cl):
        o_ref[pl.ds(t, period, stride=cl), :] = y[:, t * lanes:(t + 1) * lanes]


def _s5_scan(u, ops, *, bsz, slen, name):
    lag_blocks, w_rows, vt_rows, ars, ais, d_row = ops
    n_tiles = lag_blocks.shape[0]
    period = slen // SSM_CHUNK
    n_levels = ars.shape[1]
    width = SSM_CHUNK * V7X_LANES
    state = _GROUPS_PER_TILE * 2 * SSM_STATE
    per_tile = lambda a: pl.BlockSpec((1,) + a.shape[1:], lambda o, b: (o,) + (0,) * (a.ndim - 1))
    io_spec = pl.BlockSpec((slen, V7X_LANES), lambda o, b: (b, o))
    return pl.pallas_call(
        functools.partial(_s5_kernel, n_levels=n_levels, period=period),
        grid=(n_tiles, bsz),
        in_specs=[io_spec] + [per_tile(a) for a in (lag_blocks, w_rows, vt_rows, ars, ais, d_row)],
        out_specs=io_spec,
        out_shape=jax.ShapeDtypeStruct(u.shape, F32),
        scratch_shapes=[pltpu.VMEM((width, width), BF16),
                        pltpu.VMEM((width, state), BF16),
                        pltpu.VMEM((width, state), BF16)],
        compiler_params=_cparams(2),
        name=name,
    )(u, lag_blocks, w_rows, vt_rows, ars, ais, d_row)


def _glu_mm_kernel(g_ref, w_ref, b_ref, o_ref):
    g = g_ref[...]
    z = _dot(g.astype(w_ref.dtype), w_ref[...]) + b_ref[...]
    o_ref[...] = (g * jax.nn.sigmoid(z)).astype(o_ref.dtype)


def _glu_matmul(g, w, b, *, tm, name):
    m, k = g.shape
    n = w.shape[1]
    return pl.pallas_call(
        _glu_mm_kernel,
        grid=(m // tm,),
        in_specs=[pl.BlockSpec((tm, k), lambda i: (i, 0)),
                  pl.BlockSpec((k, n), lambda i: (0, 0)),
                  pl.BlockSpec((1, n), lambda i: (0, 0))],
        out_specs=pl.BlockSpec((tm, n), lambda i: (i, 0)),
        out_shape=jax.ShapeDtypeStruct((m, n), BF16),
        compiler_params=_cparams(1),
        name=name,
    )(g, w, b.reshape(1, n))


def _s5_mixer(u, lre, lim, log_step, bre, bim, cre, cim, dd, w_glu, b_glu, *, bsz, slen, name):
    period = slen // SSM_CHUNK
    n_levels = period.bit_length() - 1
    assert (1 << n_levels) == period and slen % SSM_CHUNK == 0
    ops = _s5_operators(lre, lim, log_step, bre, bim, cre, cim, dd, n_levels)
    g_act = _s5_scan(u.astype(F32), ops, bsz=bsz, slen=slen, name=name)
    return _glu_matmul(g_act, w_glu.astype(BF16), b_glu.astype(F32), tm=1024, name=name + "_glu")


_CONV_HALO = 32
_CONV_ROWS = 32
_CONV_COLS = 512


def _conv_ln_kernel(prev_ref, cur_ref, w_ref, b_ref, lg_ref, lb_ref, o_ref, ext_ref, y_ref, *, ts):
    n_shift = V7X_SUBLANES
    kw = w_ref.shape[0] // n_shift
    n_ch = cur_ref.shape[-1]
    span = ext_ref.shape[1]
    first = pl.program_id(1) == 0
    prev = prev_ref[0].astype(F32)
    ext_ref[0, 0:_CONV_HALO, :] = jnp.where(first, jnp.zeros_like(prev), prev)
    ext_ref[0, _CONV_HALO:, :] = cur_ref[0].astype(F32)
    for r in range(1, n_shift):
        ext_ref[r, 0:span - n_shift, :] = ext_ref[0, r:span - n_shift + r, :]
    base = _CONV_HALO - (kw - 1)
    for c0 in range(0, n_ch, _CONV_COLS):
        cols = slice(c0, c0 + _CONV_COLS)
        for r0 in range(0, ts, _CONV_ROWS):
            acc = jnp.broadcast_to(b_ref[:, cols], (_CONV_ROWS, _CONV_COLS))
            for k in range(kw):
                off = r0 + base + k
                lo = off - off % n_shift
                tap = jnp.tile(w_ref[k * n_shift:(k + 1) * n_shift, cols], (_CONV_ROWS // n_shift, 1))
                acc = acc + tap * ext_ref[off % n_shift, lo:lo + _CONV_ROWS, cols]
            y_ref[r0:r0 + _CONV_ROWS, cols] = acc
    y = y_ref[...]
    mu = jnp.mean(y, axis=-1, keepdims=True)
    yc = y - mu
    var = jnp.mean(yc * yc, axis=-1, keepdims=True)
    r = yc * lax.rsqrt(var + EPS) * lg_ref[...] + lb_ref[...]
    o_ref[0] = (r * jax.nn.sigmoid(r)).astype(o_ref.dtype)


def _conv_ln_silu(z3, w_dw, b_dw, ln_g, ln_b, *, ts, name):
    bsz, slen, n_ch = z3.shape
    kw = w_dw.shape[0]
    assert kw - 1 <= _CONV_HALO
    row = lambda a: a.astype(F32).reshape(1, n_ch)
    full = lambda r: pl.BlockSpec((r, n_ch), lambda b, i: (0, 0))
    per_halo = ts // _CONV_HALO
    w_rows = jnp.repeat(w_dw.astype(F32), V7X_SUBLANES, axis=0)
    return pl.pallas_call(
        functools.partial(_conv_ln_kernel, ts=ts),
        grid=(bsz, slen // ts),
        in_specs=[pl.BlockSpec((1, _CONV_HALO, n_ch),
                               lambda b, i: (b, jnp.maximum(i * per_halo - 1, 0), 0)),
                  pl.BlockSpec((1, ts, n_ch), lambda b, i: (b, i, 0)),
                  full(kw * V7X_SUBLANES), full(1), full(1), full(1)],
        out_specs=pl.BlockSpec((1, ts, n_ch), lambda b, i: (b, i, 0)),
        out_shape=jax.ShapeDtypeStruct((bsz, slen, n_ch), BF16),
        scratch_shapes=[pltpu.VMEM((V7X_SUBLANES, ts + _CONV_HALO, n_ch), F32),
                        pltpu.VMEM((ts, n_ch), F32)],
        compiler_params=_cparams(2),
        name=name,
    )(z3, z3, w_rows, row(b_dw), row(ln_g), row(ln_b))


def _ffn_kernel(h_ref, wg_ref, wv_ref, cg_ref, cv_ref, bg_ref, bv_ref, wd_ref, x_ref, o_ref,
                act_ref, carry_ref, *, n_up, tiles_per_seq):
    i = pl.program_id(0)
    j = pl.program_id(1)
    tm = h_ref.shape[0]
    th = wg_ref.shape[1]
    keep = V7X_SUBLANES

    @pl.when(j < n_up)
    def _():
        @pl.when(i % tiles_per_seq == 0)
        def _():
            carry_ref[j] = jnp.zeros(carry_ref.shape[1:], carry_ref.dtype)

        h = h_ref[...]

        def conv_half(w_ref, c_ref, b_ref, slot):
            up = _dot(h, w_ref[...])
            prev = carry_ref[j, slot]
            carry_ref[j, slot] = up[tm - keep:, :]
            p1 = prev[keep - 1:keep]
            p2 = prev[keep - 2:keep - 1]
            row = lax.broadcasted_iota(jnp.int32, up.shape, 0)
            x1 = jnp.where(row == 0, p1, pltpu.roll(up, 1, axis=0))
            x2 = jnp.where(row == 0, p2, jnp.where(row == 1, p1, pltpu.roll(up, 2, axis=0)))
            return c_ref[0:1] * x2 + c_ref[1:2] * x1 + c_ref[2:3] * up + b_ref[...]

        gate = conv_half(wg_ref, cg_ref, bg_ref, 0)
        val = conv_half(wv_ref, cv_ref, bv_ref, 1)
        act = gate * jax.nn.sigmoid(gate) * val
        act_ref[:, pl.ds(pl.multiple_of(j * th, th), th)] = act.astype(act_ref.dtype)

    @pl.when(j >= n_up)
    def _():
        o_ref[...] = x_ref[...] + _dot(act_ref[...], wd_ref[...])


def _conv_ffn(x, h, w_up, w_dw, b_dw, w_down, layer, *, bsz, slen, tm=1024, th=512, tn=512, name):
    m, d_model = x.shape
    hidden = w_down.shape[1]
    assert w_dw.shape[1] == 3 and slen % tm == 0 and hidden % th == 0 and d_model % tn == 0
    n_up = hidden // th
    n_dn = d_model // tn
    conv_w = w_dw.astype(F32)
    conv_b = b_dw.astype(F32).reshape(b_dw.shape[0], 1, 2 * hidden)
    up_block = lambda j: jnp.minimum(j, n_up - 1)
    dn_block = lambda j: jnp.maximum(j - n_up, 0)
    up_spec = lambda off: _layer_spec(layer, (d_model, th), lambda i, j: (0, up_block(j) + off))
    par_spec = lambda r, off: _layer_spec(layer, (r, th), lambda i, j: (0, up_block(j) + off))
    io_spec = pl.BlockSpec((tm, tn), lambda i, j: (i, dn_block(j)))
    return pl.pallas_call(
        functools.partial(_ffn_kernel, n_up=n_up, tiles_per_seq=slen // tm),
        grid=(m // tm, n_up + n_dn),
        in_specs=[pl.BlockSpec((tm, d_model), lambda i, j: (i, 0)),
                  up_spec(0), up_spec(n_up),
                  par_spec(3, 0), par_spec(3, n_up), par_spec(1, 0), par_spec(1, n_up),
                  _layer_spec(layer, (hidden, tn), lambda i, j: (0, dn_block(j))),
                  io_spec],
        out_specs=io_spec,
        out_shape=jax.ShapeDtypeStruct((m, d_model), F32),
        scratch_shapes=[pltpu.VMEM((tm, hidden), BF16),
                        pltpu.VMEM((n_up, 2, V7X_SUBLANES, th), F32)],
        compiler_params=_cparams(2),
        name=name,
    )(h, w_up, w_up, conv_w, conv_w, conv_b, conv_b, w_down, x)


def _rmsnorm_kernel(x_ref, g_ref, o_ref):
    o_ref[...] = _rms_scale(x_ref[...], g_ref[...])


def _rmsnorm(x, g, *, tm, name):
    m, k = x.shape
    return pl.pallas_call(
        _rmsnorm_kernel,
        grid=(m // tm,),
        in_specs=[pl.BlockSpec((tm, k), lambda i: (i, 0)),
                  pl.BlockSpec((1, k), lambda i: (0, 0))],
        out_specs=pl.BlockSpec((tm, k), lambda i: (i, 0)),
        out_shape=jax.ShapeDtypeStruct((m, k), F32),
        compiler_params=_cparams(1),
        name=name,
    )(x, g.astype(F32).reshape(1, k))


def kernel(x, rel_bias, norm_mix, norm_ffn, norm_final, ab_w_in, ab_w_out, diff_lq1, diff_lk1, diff_lq2, diff_lk2, diff_head_norm, s5_lambda_re, s5_lambda_im, s5_log_step, s5_b_re, s5_b_im, s5_c_re, s5_c_im, s5_d, s5_w_glu, s5_b_glu, conv_w_pw1, conv_w_dw, conv_b_dw, conv_ln_g, conv_ln_b, conv_w_pw2, ffn_w_up, ffn_w_dw, ffn_b_dw, ffn_w_down):
    bsz, slen, d_model = x.shape
    depth = norm_mix.shape[0]
    m = bsz * slen
    att_width = N_HEADS * HEAD_WIDTH

    col_scale = jnp.where(jnp.arange(ab_w_in.shape[-1]) < att_width,
                          LOG2E * HEAD_DIM ** -0.5, 1.0).astype(F32)
    w_in_all = (ab_w_in.astype(F32) * col_scale).astype(BF16)
    w_out_all = ab_w_out.astype(BF16)
    w_pw1_all = conv_w_pw1.astype(BF16)
    w_pw2_all = conv_w_pw2.astype(BF16)
    w_up_all = ffn_w_up.astype(BF16)
    w_down_all = ffn_w_down.astype(BF16)

    xf = x.astype(F32).reshape(m, d_model)
    for layer in range(depth):
        i = layer // 2
        g_mix = norm_mix[layer].astype(F32)
        if layer % 2 == 0:
            qkv, u = _norm_matmul_split(xf, g_mix, w_in_all, i, n_a=3 * att_width, tm=1024, tn=1024,
                                        name=f"in_proj_{layer}")
            lam_init = 0.8 - 0.6 * math.exp(-0.3 * layer)
            att = _diff_attention(qkv.reshape(bsz, slen, -1), rel_bias, diff_lq1[i], diff_lk1[i],
                                  diff_lq2[i], diff_lk2[i], diff_head_norm[i],
                                  lam_init=lam_init, t=512, name=f"diff_attn_{layer}")
            ssm = _s5_mixer(u, s5_lambda_re[i], s5_lambda_im[i], s5_log_step[i], s5_b_re[i],
                            s5_b_im[i], s5_c_re[i], s5_c_im[i], s5_d[i], s5_w_glu[i], s5_b_glu[i],
                            bsz=bsz, slen=slen, name=f"s5_{layer}")
            xf, h_ffn = _matmul_residual_norm(
                [(att.reshape(m, att_width), 0), (ssm, 1)], w_out_all, i, xf, norm_ffn[layer],
                tm=512, name=f"out_proj_{layer}")
        else:
            z = _norm_glu_matmul(xf, g_mix, w_pw1_all, i, tm=1024, tn=1024, name=f"conv_pw1_{layer}")
            c = _conv_ln_silu(z.reshape(bsz, slen, d_model), conv_w_dw[i], conv_b_dw[i],
                              conv_ln_g[i], conv_ln_b[i], ts=256, name=f"conv_dw_{layer}")
            xf, h_ffn = _matmul_residual_norm(
                [(c.reshape(m, d_model), 0)], w_pw2_all, i, xf, norm_ffn[layer],
                tm=512, name=f"conv_pw2_{layer}")
        xf = _conv_ffn(xf, h_ffn, w_up_all, ffn_w_dw, ffn_b_dw, w_down_all, layer,
                       bsz=bsz, slen=slen, name=f"ffn_{layer}")
    out = _rmsnorm(xf, norm_final, tm=512, name="final_norm")
    return out.reshape(bsz, slen, d_model).astype(x.dtype)
```

```python
import functools
import math

import numpy as np
import jax
import jax.numpy as jnp
from jax import lax
from jax.experimental import pallas as pl
from jax.experimental.pallas import tpu as pltpu

F32 = jnp.float32
BF16 = jnp.bfloat16

N_HEADS = 8
HEAD_DIM = 64
HEAD_WIDTH = 2 * HEAD_DIM
N_BUCKETS = 32
MAX_DISTANCE = 128
SSM_GROUP = 16
SSM_STATE = 64
SSM_CHUNK = 16
EPS = 1e-6

V7X_LANES = 128
V7X_SUBLANES = 8
V7X_VMEM_LIMIT_BYTES = 56 * 1024 * 1024
MASK_VALUE = -1e30
LOG2E = math.log2(math.e)

_NT_DIMS = (((1,), (1,)), ((), ()))


def _cparams(n_axes):
    return pltpu.CompilerParams(
        dimension_semantics=("arbitrary",) * n_axes,
        vmem_limit_bytes=V7X_VMEM_LIMIT_BYTES)


def _dot(a, b):
    return jnp.dot(a, b, preferred_element_type=F32)


def _rms_scale(x, g):
    ms = jnp.mean(x * x, axis=-1, keepdims=True)
    return x * lax.rsqrt(ms + EPS) * g


def _by_parity(n, fn):
    @pl.when(n % 2 == 0)
    def _():
        fn(0)

    @pl.when(n % 2 == 1)
    def _():
        fn(1)


def _static_lookup(table, idx):
    idx = np.asarray(idx)
    values = sorted(set(idx.ravel().tolist()))
    col = lambda b: table[b].reshape((-1,) + (1,) * idx.ndim)
    out = jnp.broadcast_to(col(values[-1]), (table.shape[1],) + idx.shape)
    for b in values[-2::-1]:
        out = jnp.where(jnp.asarray(idx == b)[None], col(b), out)
    return out


def _layer_spec(layer, block, index_map):
    return pl.BlockSpec((None,) + block, lambda *idx: (layer,) + index_map(*idx))


def _norm_glu_kernel(x_ref, g_ref, wa_ref, wb_ref, o_ref, h_ref):
    @pl.when(pl.program_id(1) == 0)
    def _():
        h_ref[...] = _rms_scale(x_ref[...], g_ref[...]).astype(h_ref.dtype)

    h = h_ref[...]
    a = _dot(h, wa_ref[...])
    gate = _dot(h, wb_ref[...])
    o_ref[...] = (a * jax.nn.sigmoid(gate)).astype(o_ref.dtype)


def _norm_glu_matmul(x, g, w_all, layer, *, tm, tn, name):
    m, k = x.shape
    n = w_all.shape[2] // 2
    return pl.pallas_call(
        _norm_glu_kernel,
        grid=(m // tm, n // tn),
        in_specs=[pl.BlockSpec((tm, k), lambda i, j: (i, 0)),
                  pl.BlockSpec((1, k), lambda i, j: (0, 0)),
                  _layer_spec(layer, (k, tn), lambda i, j: (0, j)),
                  _layer_spec(layer, (k, tn), lambda i, j: (0, j + n // tn))],
        out_specs=pl.BlockSpec((tm, tn), lambda i, j: (i, j)),
        out_shape=jax.ShapeDtypeStruct((m, n), BF16),
        scratch_shapes=[pltpu.VMEM((tm, k), BF16)],
        compiler_params=_cparams(2),
        name=name,
    )(x, g.reshape(1, k), w_all, w_all)


def _norm_split_kernel(x_ref, g_ref, w_ref, a_ref, b_ref, h_ref, *, n_first):
    j = pl.program_id(1)

    @pl.when(j == 0)
    def _():
        h_ref[...] = _rms_scale(x_ref[...], g_ref[...]).astype(h_ref.dtype)

    r = _dot(h_ref[...], w_ref[...])

    @pl.when(j < n_first)
    def _():
        a_ref[...] = r.astype(a_ref.dtype)

    @pl.when(j >= n_first)
    def _():
        b_ref[...] = r.astype(b_ref.dtype)


def _norm_matmul_split(x, g, w_all, layer, *, n_a, tm, tn, name):
    m, k = x.shape
    n = w_all.shape[2]
    n_first = n_a // tn
    return pl.pallas_call(
        functools.partial(_norm_split_kernel, n_first=n_first),
        grid=(m // tm, n // tn),
        in_specs=[pl.BlockSpec((tm, k), lambda i, j: (i, 0)),
                  pl.BlockSpec((1, k), lambda i, j: (0, 0)),
                  _layer_spec(layer, (k, tn), lambda i, j: (0, j))],
        out_specs=[pl.BlockSpec((tm, tn), lambda i, j: (i, jnp.minimum(j, n_first - 1))),
                   pl.BlockSpec((tm, tn), lambda i, j: (i, jnp.maximum(j - n_first, 0)))],
        out_shape=[jax.ShapeDtypeStruct((m, n_a), BF16), jax.ShapeDtypeStruct((m, n - n_a), F32)],
        scratch_shapes=[pltpu.VMEM((tm, k), BF16)],
        compiler_params=_cparams(2),
        name=name,
    )(x, g.reshape(1, k), w_all)


def _mm_res_norm_kernel(*refs, n_pairs):
    res_ref, g_ref, o_ref, h_ref = refs[2 * n_pairs:]
    acc = res_ref[...]
    for p in range(n_pairs):
        acc = acc + _dot(refs[2 * p][...], refs[2 * p + 1][...])
    o_ref[...] = acc
    h_ref[...] = _rms_scale(acc, g_ref[...]).astype(h_ref.dtype)


def _matmul_residual_norm(pairs, w_all, layer, res, g, *, tm, name):
    m, n = res.shape
    in_specs, args = [], []
    for a, row_block in pairs:
        k = a.shape[1]
        in_specs += [pl.BlockSpec((tm, k), lambda i: (i, 0)),
                     _layer_spec(layer, (k, n), functools.partial(lambda i, rb: (rb, 0), rb=row_block))]
        args += [a, w_all]
    in_specs += [pl.BlockSpec((tm, n), lambda i: (i, 0)), pl.BlockSpec((1, n), lambda i: (0, 0))]
    args += [res, g.astype(F32).reshape(1, n)]
    return pl.pallas_call(
        functools.partial(_mm_res_norm_kernel, n_pairs=len(pairs)),
        grid=(m // tm,),
        in_specs=in_specs,
        out_specs=[pl.BlockSpec((tm, n), lambda i: (i, 0)), pl.BlockSpec((tm, n), lambda i: (i, 0))],
        out_shape=[jax.ShapeDtypeStruct((m, n), F32), jax.ShapeDtypeStruct((m, n), BF16)],
        compiler_params=_cparams(1),
        name=name,
    )(*args)


def _t5_bucket_np(rel):
    n = np.maximum(rel, 0)
    max_exact = N_BUCKETS // 2
    nf = np.maximum(n, 1).astype(np.float32)
    large = max_exact + (np.log(nf / np.float32(max_exact))
                         / np.float32(math.log(MAX_DISTANCE / max_exact))
                         * np.float32(N_BUCKETS - max_exact)).astype(np.int32)
    large = np.minimum(large, N_BUCKETS - 1)
    return np.where(n < max_exact, n, large)


def _bias_tables(rel_bias, slen, t):
    i = np.arange(t)[:, None]
    j = np.arange(t)[None, :]
    far = _t5_bucket_np(np.arange(t + 1, max(slen, t + 2)))
    far_bucket = int(far[0])
    assert np.all(far == far_bucket)
    rb = rel_bias.astype(F32)
    rb = (rb - rb[far_bucket]) * LOG2E
    d0 = jnp.where(jnp.asarray(i >= j)[None], _static_lookup(rb, _t5_bucket_np(i - j)), MASK_VALUE)
    d1 = _static_lookup(rb, _t5_bucket_np(i - j + t))
    return jnp.stack([d0, d1], axis=1)


def _attn_kernel(q_ref, k_ref, v_ref, d_ref, lq1_ref, lk1_ref, lq2_ref, lk2_ref, hn_ref, o_ref,
                 vaug_ref, qs_ref, s_ref, acc_ref, m_ref, *, t, lam_init):
    qi = pl.program_id(2)

    @pl.when(qi == 0)
    def _():
        vaug_ref[:, :HEAD_WIDTH] = v_ref[0]
        vaug_ref[:, HEAD_WIDTH:] = jnp.ones((vaug_ref.shape[0], HEAD_WIDTH), vaug_ref.dtype)

    q = q_ref[0]
    lane = lax.broadcasted_iota(jnp.int32, q.shape, 1)
    zero = jnp.zeros_like(q)
    qs_ref[:t] = jnp.where(lane < HEAD_DIM, q, zero)
    qs_ref[t:] = jnp.where(lane >= HEAD_DIM, q, zero)

    acc_ref[...] = jnp.zeros_like(acc_ref)
    m_ref[...] = jnp.full_like(m_ref, MASK_VALUE)

    def scores(blk, slot):
        kb = k_ref[0, pl.ds(pl.multiple_of(blk * t, t), t), :]
        s_ref[slot] = lax.dot_general(qs_ref[...], kb, _NT_DIMS, preferred_element_type=F32)

    def update(blk, slot, bias):
        s = s_ref[slot]
        if bias is not None:
            s = jnp.concatenate([s[:t] + bias, s[t:] + bias], axis=0)
        vb = vaug_ref[pl.ds(pl.multiple_of(blk * t, t), t), :]
        m_old = m_ref[...]
        m_new = jnp.maximum(m_old, jnp.max(s, axis=-1, keepdims=True))
        alpha = jnp.exp2(m_old - m_new)
        p = jnp.exp2(s - jnp.concatenate([m_new] * (t // HEAD_WIDTH), axis=1))
        pv = _dot(p.astype(vb.dtype), vb)
        acc_ref[...] = jnp.concatenate([alpha, alpha], axis=1) * acc_ref[...] + pv
        m_ref[...] = m_new

    scores(0, 0)

    def far_body(blk, carry):
        _by_parity(blk, lambda cur: (scores(blk + 1, 1 - cur), update(blk, cur, None)))
        return carry

    lax.fori_loop(0, qi - 1, far_body, 0)

    @pl.when(qi >= 1)
    def _():
        _by_parity(qi - 1, lambda cur: (scores(qi, 1 - cur), update(qi - 1, cur, d_ref[0, 1])))

    _by_parity(qi, lambda cur: update(qi, cur, d_ref[0, 0]))

    lam = (jnp.exp(jnp.sum(lq1_ref[...] * lk1_ref[...], keepdims=True))
           - jnp.exp(jnp.sum(lq2_ref[...] * lk2_ref[...], keepdims=True)) + lam_init)
    a1 = acc_ref[:t]
    a2 = acc_ref[t:]
    o = a1[:, :HEAD_WIDTH] / a1[:, HEAD_WIDTH:] - lam * (a2[:, :HEAD_WIDTH] / a2[:, HEAD_WIDTH:])
    o_ref[0] = (_rms_scale(o, hn_ref[...]) * (1.0 - lam_init)).astype(o_ref.dtype)


_N_ATTN_INPUTS = 9


def _attn_cast_kernel(*refs, n_cast, t, lam_init):
    ins = refs[:_N_ATTN_INPUTS]
    cast_in = refs[_N_ATTN_INPUTS:_N_ATTN_INPUTS + n_cast]
    o_ref = refs[_N_ATTN_INPUTS + n_cast]
    cast_out = refs[_N_ATTN_INPUTS + n_cast + 1:_N_ATTN_INPUTS + 2 * n_cast + 1]
    scratch = refs[_N_ATTN_INPUTS + 2 * n_cast + 1:]
    for src, dst in zip(cast_in, cast_out):
        dst[...] = src[...].astype(dst.dtype)
    _attn_kernel(*ins, o_ref, *scratch, t=t, lam_init=lam_init)


def _cast_slab_spec(w, n_steps, nq):
    n_layers, rows, cols = w.shape
    steps_per_layer = n_steps // n_layers
    assert n_steps % n_layers == 0
    n_blocks = max(d for d in range(1, steps_per_layer + 1)
                   if steps_per_layer % d == 0 and rows % (d * 2 * V7X_SUBLANES) == 0)
    repeat = steps_per_layer // n_blocks

    def index(b, h, qi):
        step = (b * N_HEADS + h) * nq + qi
        return (step // steps_per_layer, (step % steps_per_layer) // repeat, 0)

    return pl.BlockSpec((1, rows // n_blocks, cols), index)


def _diff_attention(qkv3, rel_bias, lq1, lk1, lq2, lk2, head_norm, *, lam_init, t, name, casts=()):
    bsz, slen, _ = qkv3.shape
    nq = slen // t
    tiles = _bias_tables(rel_bias, slen, t)
    vec = lambda a: a.astype(F32).reshape(1, -1)
    small = lambda n: pl.BlockSpec((1, n), lambda b, h, qi: (0, 0))
    cast_specs = [_cast_slab_spec(w, bsz * N_HEADS * nq, nq) for w in casts]
    out = pl.pallas_call(
        functools.partial(_attn_cast_kernel, n_cast=len(casts), t=t, lam_init=lam_init),
        grid=(bsz, N_HEADS, nq),
        in_specs=[pl.BlockSpec((1, t, HEAD_WIDTH), lambda b, h, qi: (b, qi, h)),
                  pl.BlockSpec((1, slen, HEAD_WIDTH), lambda b, h, qi: (b, 0, N_HEADS + h)),
                  pl.BlockSpec((1, slen, HEAD_WIDTH), lambda b, h, qi: (b, 0, 2 * N_HEADS + h)),
                  pl.BlockSpec((1, 2, t, t), lambda b, h, qi: (h, 0, 0, 0)),
                  small(HEAD_DIM), small(HEAD_DIM), small(HEAD_DIM), small(HEAD_DIM),
                  small(HEAD_WIDTH)] + cast_specs,
        out_specs=[pl.BlockSpec((1, t, HEAD_WIDTH), lambda b, h, qi: (b, qi, h))] + cast_specs,
        out_shape=[jax.ShapeDtypeStruct((bsz, slen, N_HEADS * HEAD_WIDTH), BF16)]
        + [jax.ShapeDtypeStruct(w.shape, BF16) for w in casts],
        scratch_shapes=[pltpu.VMEM((slen, 2 * HEAD_WIDTH), BF16),
                        pltpu.VMEM((2 * t, HEAD_WIDTH), BF16),
                        pltpu.VMEM((2, 2 * t, t), F32),
                        pltpu.VMEM((2 * t, 2 * HEAD_WIDTH), F32),
                        pltpu.VMEM((2 * t, HEAD_WIDTH), F32)],
        compiler_params=_cparams(3),
        name=name,
    )(qkv3, qkv3, qkv3, tiles, vec(lq1), vec(lk1), vec(lq2), vec(lk2), vec(head_norm), *casts)
    return out[0] if not casts else out


_GROUPS_PER_TILE = V7X_LANES // SSM_GROUP


def _s5_operators(lre, lim, log_step, bre, bim, cre, cim, dd, n_levels):
    f = lambda a: a.astype(F32)
    lre, lim, bre, bim, cre, cim = map(f, (lre, lim, bre, bim, cre, cim))
    n_groups = lre.shape[0]
    cl, gt = SSM_CHUNK, _GROUPS_PER_TILE
    n_tiles = n_groups // gt
    hi = lax.Precision.HIGHEST
    dt = jnp.exp(f(log_step))[:, None]
    zr, zi = lre * dt, lim * dt
    steps = jnp.arange(cl + 1, dtype=F32)[None, :, None]
    mag = jnp.exp(zr[:, None, :] * steps)
    ang = zi[:, None, :] * steps
    pr, pi = mag * jnp.cos(ang), mag * jnp.sin(ang)
    nr = jnp.expm1(zr) * jnp.cos(zi) - 2.0 * jnp.sin(0.5 * zi) ** 2
    ni = jnp.exp(zr) * jnp.sin(zi)
    den = lre * lre + lim * lim
    fr, fi = (nr * lre + ni * lim) / den, (ni * lre - nr * lim) / den
    bbr = fr[..., None] * bre - fi[..., None] * bim
    bbi = fr[..., None] * bim + fi[..., None] * bre
    cpr = cre[:, None] * pr[:, :, None, :] - cim[:, None] * pi[:, :, None, :]
    cpi = cre[:, None] * pi[:, :, None, :] + cim[:, None] * pr[:, :, None, :]
    kern = (jnp.einsum('gjhp,gpi->gjhi', cpr[:, :cl], bbr, precision=hi)
            - jnp.einsum('gjhp,gpi->gjhi', cpi[:, :cl], bbi, precision=hi))
    eye = jnp.eye(gt, dtype=F32)
    tile = lambda a: a.reshape((n_tiles, gt) + a.shape[1:])
    lag_blocks = jnp.einsum('ogjhi,gk->ojgikh', tile(kern), eye, precision=hi)
    lag_blocks = lag_blocks.reshape(n_tiles, cl, V7X_LANES, V7X_LANES)
    pr_t, pi_t = tile(pr).transpose(0, 2, 1, 3), tile(pi).transpose(0, 2, 1, 3)
    rev = np.arange(cl - 1, -1, -1)
    prr, pir = pr_t[:, rev][:, :, :, None, :], pi_t[:, rev][:, :, :, None, :]
    bbr_t = tile(bbr.transpose(0, 2, 1))[:, None]
    bbi_t = tile(bbi.transpose(0, 2, 1))[:, None]
    w_rows = jnp.concatenate([prr * bbr_t - pir * bbi_t, prr * bbi_t + pir * bbr_t], axis=-1)
    w_rows = w_rows.reshape(n_tiles, cl * V7X_LANES, 2 * SSM_STATE)
    ptr, pti = pr_t[:, 1:][:, :, :, None, :], pi_t[:, 1:][:, :, :, None, :]
    cre_t, cim_t = tile(cre)[:, None], tile(cim)[:, None]
    vt_rows = jnp.concatenate([cre_t * ptr - cim_t * pti, -(cre_t * pti + cim_t * ptr)], axis=-1)
    vt_rows = vt_rows.reshape(n_tiles, cl * V7X_LANES, 2 * SSM_STATE)
    ar, ai = pr[:, cl], pi[:, cl]
    ars, ais = [], []
    for _ in range(n_levels):
        ars.append(jnp.concatenate([ar, ar], axis=-1).reshape(n_tiles, -1))
        ais.append(jnp.concatenate([-ai, ai], axis=-1).reshape(n_tiles, -1))
        ar, ai = ar * ar - ai * ai, 2.0 * ar * ai
    d_row = jnp.tile(f(dd).reshape(n_tiles, V7X_LANES), (1, cl))[:, None, :]
    return (lag_blocks.astype(BF16), w_rows.astype(BF16), vt_rows.astype(BF16),
            jnp.stack(ars, axis=1), jnp.stack(ais, axis=1), d_row)


def _s5_kernel(u_ref, bd_ref, wr_ref, vr_ref, ar_ref, ai_ref, d_ref, o_ref,
               t_ref, w_ref, vt_ref, *, n_levels, period):
    cl, gt, lanes = SSM_CHUNK, _GROUPS_PER_TILE, V7X_LANES

    @pl.when(pl.program_id(1) == 0)
    def _():
        zeros = jnp.zeros((lanes, lanes), t_ref.dtype)
        for s in range(cl):
            for t in range(cl):
                t_ref[s * lanes:(s + 1) * lanes, t * lanes:(t + 1) * lanes] = (
                    bd_ref[0, t - s] if t >= s else zeros)
        w_rows = wr_ref[0].astype(F32)
        v_rows = vr_ref[0].astype(F32)
        group = (lax.broadcasted_iota(jnp.int32, w_rows.shape, 0) // SSM_GROUP) % gt
        for g in range(gt):
            cols = slice(g * lanes, (g + 1) * lanes)
            w_ref[:, cols] = jnp.where(group == g, w_rows, 0.0).astype(w_ref.dtype)
            vt_ref[:, cols] = jnp.where(group == g, v_rows, 0.0).astype(vt_ref.dtype)

    u32 = jnp.concatenate([u_ref[pl.ds(s, period, stride=cl), :] for s in range(cl)], axis=1)
    u = u32.astype(BF16)
    x = _dot(u, w_ref[...])
    row = lax.broadcasted_iota(jnp.int32, x.shape, 0)

    def swap_re_im(a):
        return jnp.concatenate(
            [pltpu.roll(a[:, g * lanes:(g + 1) * lanes], SSM_STATE, axis=1) for g in range(gt)], axis=1)

    for k in range(n_levels):
        d = 1 << k
        xs = jnp.where(row >= d, pltpu.roll(x, d, axis=0), 0.0)
        x = x + ar_ref[0, k:k + 1, :] * xs + ai_ref[0, k:k + 1, :] * swap_re_im(xs)
    xp = jnp.where(row >= 1, pltpu.roll(x, 1, axis=0), 0.0)
    xh = xp.astype(BF16)
    xl = (xp - xh.astype(F32)).astype(BF16)
    vt = vt_ref[...]
    nt_dot = lambda a, b: lax.dot_general(a, b, _NT_DIMS, preferred_element_type=F32)
    y = _dot(u, t_ref[...]) + nt_dot(xh, vt) + nt_dot(xl, vt) + d_ref[0] * u32
    y = jax.nn.gelu(y)
    for t in range(cl):
        o_ref[pl.ds(t, period, stride=cl), :] = y[:, t * lanes:(t + 1) * lanes]


def _s5_scan(u, ops, *, bsz, slen, name):
    lag_blocks, w_rows, vt_rows, ars, ais, d_row = ops
    n_tiles = lag_blocks.shape[0]
    period = slen // SSM_CHUNK
    n_levels = ars.shape[1]
    width = SSM_CHUNK * V7X_LANES
    state = _GROUPS_PER_TILE * 2 * SSM_STATE
    per_tile = lambda a: pl.BlockSpec((1,) + a.shape[1:], lambda o, b: (o,) + (0,) * (a.ndim - 1))
    io_spec = pl.BlockSpec((slen, V7X_LANES), lambda o, b: (b, o))
    return pl.pallas_call(
        functools.partial(_s5_kernel, n_levels=n_levels, period=period),
        grid=(n_tiles, bsz),
        in_specs=[io_spec] + [per_tile(a) for a in (lag_blocks, w_rows, vt_rows, ars, ais, d_row)],
        out_specs=io_spec,
        out_shape=jax.ShapeDtypeStruct(u.shape, F32),
        scratch_shapes=[pltpu.VMEM((width, width), BF16),
                        pltpu.VMEM((width, state), BF16),
                        pltpu.VMEM((width, state), BF16)],
        compiler_params=_cparams(2),
        name=name,
    )(u, lag_blocks, w_rows, vt_rows, ars, ais, d_row)


def _glu_mm_kernel(g_ref, w_ref, b_ref, o_ref):
    g = g_ref[...]
    z = _dot(g.astype(w_ref.dtype), w_ref[...]) + b_ref[...]
    o_ref[...] = (g * jax.nn.sigmoid(z)).astype(o_ref.dtype)


def _glu_matmul(g, w, b, *, tm, name):
    m, k = g.shape
    n = w.shape[1]
    return pl.pallas_call(
        _glu_mm_kernel,
        grid=(m // tm,),
        in_specs=[pl.BlockSpec((tm, k), lambda i: (i, 0)),
                  pl.BlockSpec((k, n), lambda i: (0, 0)),
                  pl.BlockSpec((1, n), lambda i: (0, 0))],
        out_specs=pl.BlockSpec((tm, n), lambda i: (i, 0)),
        out_shape=jax.ShapeDtypeStruct((m, n), BF16),
        compiler_params=_cparams(1),
        name=name,
    )(g, w, b.reshape(1, n))


def _s5_mixer(u, lre, lim, log_step, bre, bim, cre, cim, dd, w_glu, b_glu, *, bsz, slen, name):
    period = slen // SSM_CHUNK
    n_levels = period.bit_length() - 1
    assert (1 << n_levels) == period and slen % SSM_CHUNK == 0
    ops = _s5_operators(lre, lim, log_step, bre, bim, cre, cim, dd, n_levels)
    g_act = _s5_scan(u.astype(F32), ops, bsz=bsz, slen=slen, name=name)
    return _glu_matmul(g_act, w_glu.astype(BF16), b_glu.astype(F32), tm=1024, name=name + "_glu")


_CONV_HALO = 32
_CONV_ROWS = 32
_CONV_COLS = 512


def _conv_ln_kernel(prev_ref, cur_ref, w_ref, b_ref, lg_ref, lb_ref, o_ref, ext_ref, y_ref, *, ts):
    n_shift = V7X_SUBLANES
    kw = w_ref.shape[0] // n_shift
    n_ch = cur_ref.shape[-1]
    span = ext_ref.shape[1]
    first = pl.program_id(1) == 0
    prev = prev_ref[0].astype(F32)
    ext_ref[0, 0:_CONV_HALO, :] = jnp.where(first, jnp.zeros_like(prev), prev)
    ext_ref[0, _CONV_HALO:, :] = cur_ref[0].astype(F32)
    for r in range(1, n_shift):
        ext_ref[r, 0:span - n_shift, :] = ext_ref[0, r:span - n_shift + r, :]
    base = _CONV_HALO - (kw - 1)
    for c0 in range(0, n_ch, _CONV_COLS):
        cols = slice(c0, c0 + _CONV_COLS)
        for r0 in range(0, ts, _CONV_ROWS):
            acc = jnp.broadcast_to(b_ref[:, cols], (_CONV_ROWS, _CONV_COLS))
            for k in range(kw):
                off = r0 + base + k
                lo = off - off % n_shift
                tap = jnp.tile(w_ref[k * n_shift:(k + 1) * n_shift, cols], (_CONV_ROWS // n_shift, 1))
                acc = acc + tap * ext_ref[off % n_shift, lo:lo + _CONV_ROWS, cols]
            y_ref[r0:r0 + _CONV_ROWS, cols] = acc
    y = y_ref[...]
    mu = jnp.mean(y, axis=-1, keepdims=True)
    yc = y - mu
    var = jnp.mean(yc * yc, axis=-1, keepdims=True)
    r = yc * lax.rsqrt(var + EPS) * lg_ref[...] + lb_ref[...]
    o_ref[0] = (r * jax.nn.sigmoid(r)).astype(o_ref.dtype)


def _conv_ln_silu(z3, w_dw, b_dw, ln_g, ln_b, *, ts, name):
    bsz, slen, n_ch = z3.shape
    kw = w_dw.shape[0]
    assert kw - 1 <= _CONV_HALO
    row = lambda a: a.astype(F32).reshape(1, n_ch)
    full = lambda r: pl.BlockSpec((r, n_ch), lambda b, i: (0, 0))
    per_halo = ts // _CONV_HALO
    w_rows = jnp.repeat(w_dw.astype(F32), V7X_SUBLANES, axis=0)
    return pl.pallas_call(
        functools.partial(_conv_ln_kernel, ts=ts),
        grid=(bsz, slen // ts),
        in_specs=[pl.BlockSpec((1, _CONV_HALO, n_ch),
                               lambda b, i: (b, jnp.maximum(i * per_halo - 1, 0), 0)),
                  pl.BlockSpec((1, ts, n_ch), lambda b, i: (b, i, 0)),
                  full(kw * V7X_SUBLANES), full(1), full(1), full(1)],
        out_specs=pl.BlockSpec((1, ts, n_ch), lambda b, i: (b, i, 0)),
        out_shape=jax.ShapeDtypeStruct((bsz, slen, n_ch), BF16),
        scratch_shapes=[pltpu.VMEM((V7X_SUBLANES, ts + _CONV_HALO, n_ch), F32),
                        pltpu.VMEM((ts, n_ch), F32)],
        compiler_params=_cparams(2),
        name=name,
    )(z3, z3, w_rows, row(b_dw), row(ln_g), row(ln_b))


def _ffn_kernel(h_ref, wg_ref, wv_ref, cg_ref, cv_ref, bg_ref, bv_ref, wd_ref, x_ref, o_ref,
                act_ref, carry_ref, *, n_up, tiles_per_seq):
    i = pl.program_id(0)
    j = pl.program_id(1)
    tm = h_ref.shape[0]
    th = wg_ref.shape[1]
    keep = V7X_SUBLANES

    @pl.when(j < n_up)
    def _():
        @pl.when(i % tiles_per_seq == 0)
        def _():
            carry_ref[j] = jnp.zeros(carry_ref.shape[1:], carry_ref.dtype)

        h = h_ref[...]

        def conv_half(w_ref, c_ref, b_ref, slot):
            up = _dot(h, w_ref[...])
            prev = carry_ref[j, slot]
            carry_ref[j, slot] = up[tm - keep:, :]
            p1 = prev[keep - 1:keep]
            p2 = prev[keep - 2:keep - 1]
            row = lax.broadcasted_iota(jnp.int32, up.shape, 0)
            x1 = jnp.where(row == 0, p1, pltpu.roll(up, 1, axis=0))
            x2 = jnp.where(row == 0, p2, jnp.where(row == 1, p1, pltpu.roll(up, 2, axis=0)))
            return c_ref[0:1] * x2 + c_ref[1:2] * x1 + c_ref[2:3] * up + b_ref[...]

        gate = conv_half(wg_ref, cg_ref, bg_ref, 0)
        val = conv_half(wv_ref, cv_ref, bv_ref, 1)
        act = gate * jax.nn.sigmoid(gate) * val
        act_ref[:, pl.ds(pl.multiple_of(j * th, th), th)] = act.astype(act_ref.dtype)

    @pl.when(j >= n_up)
    def _():
        o_ref[...] = x_ref[...] + _dot(act_ref[...], wd_ref[...])


def _conv_ffn(x, h, w_up, w_dw, b_dw, w_down, layer, *, bsz, slen, tm=1024, th=512, tn=512, name):
    m, d_model = x.shape
    hidden = w_down.shape[1]
    assert w_dw.shape[1] == 3 and slen % tm == 0 and hidden % th == 0 and d_model % tn == 0
    n_up = hidden // th
    n_dn = d_model // tn
    conv_w = w_dw.astype(F32)
    conv_b = b_dw.astype(F32).reshape(b_dw.shape[0], 1, 2 * hidden)
    up_block = lambda j: jnp.minimum(j, n_up - 1)
    dn_block = lambda j: jnp.maximum(j - n_up, 0)
    up_spec = lambda off: _layer_spec(layer, (d_model, th), lambda i, j: (0, up_block(j) + off))
    par_spec = lambda r, off: _layer_spec(layer, (r, th), lambda i, j: (0, up_block(j) + off))
    io_spec = pl.BlockSpec((tm, tn), lambda i, j: (i, dn_block(j)))
    return pl.pallas_call(
        functools.partial(_ffn_kernel, n_up=n_up, tiles_per_seq=slen // tm),
        grid=(m // tm, n_up + n_dn),
        in_specs=[pl.BlockSpec((tm, d_model), lambda i, j: (i, 0)),
                  up_spec(0), up_spec(n_up),
                  par_spec(3, 0), par_spec(3, n_up), par_spec(1, 0), par_spec(1, n_up),
                  _layer_spec(layer, (hidden, tn), lambda i, j: (0, dn_block(j))),
                  io_spec],
        out_specs=io_spec,
        out_shape=jax.ShapeDtypeStruct((m, d_model), F32),
        scratch_shapes=[pltpu.VMEM((tm, hidden), BF16),
                        pltpu.VMEM((n_up, 2, V7X_SUBLANES, th), F32)],
        compiler_params=_cparams(2),
        name=name,
    )(h, w_up, w_up, conv_w, conv_w, conv_b, conv_b, w_down, x)


def _rmsnorm_kernel(x_ref, g_ref, o_ref):
    o_ref[...] = _rms_scale(x_ref[...], g_ref[...])


def _rmsnorm(x, g, *, tm, name):
    m, k = x.shape
    return pl.pallas_call(
        _rmsnorm_kernel,
        grid=(m // tm,),
        in_specs=[pl.BlockSpec((tm, k), lambda i: (i, 0)),
                  pl.BlockSpec((1, k), lambda i: (0, 0))],
        out_specs=pl.BlockSpec((tm, k), lambda i: (i, 0)),
        out_shape=jax.ShapeDtypeStruct((m, k), F32),
        compiler_params=_cparams(1),
        name=name,
    )(x, g.astype(F32).reshape(1, k))


def kernel(x, rel_bias, norm_mix, norm_ffn, norm_final, ab_w_in, ab_w_out, diff_lq1, diff_lk1, diff_lq2, diff_lk2, diff_head_norm, s5_lambda_re, s5_lambda_im, s5_log_step, s5_b_re, s5_b_im, s5_c_re, s5_c_im, s5_d, s5_w_glu, s5_b_glu, conv_w_pw1, conv_w_dw, conv_b_dw, conv_ln_g, conv_ln_b, conv_w_pw2, ffn_w_up, ffn_w_dw, ffn_b_dw, ffn_w_down):
    bsz, slen, d_model = x.shape
    depth = norm_mix.shape[0]
    m = bsz * slen
    att_width = N_HEADS * HEAD_WIDTH

    col_scale = jnp.where(jnp.arange(ab_w_in.shape[-1]) < att_width,
                          LOG2E * HEAD_DIM ** -0.5, 1.0).astype(F32)
    w_in_all = (ab_w_in.astype(F32) * col_scale).astype(BF16)
    cast_later = (ab_w_out, conv_w_pw1, conv_w_pw2, ffn_w_up, ffn_w_down)
    w_out_all = w_pw1_all = w_pw2_all = w_up_all = w_down_all = None

    xf = x.astype(F32).reshape(m, d_model)
    for layer in range(depth):
        i = layer // 2
        g_mix = norm_mix[layer].astype(F32)
        if layer % 2 == 0:
            qkv, u = _norm_matmul_split(xf, g_mix, w_in_all, i, n_a=3 * att_width, tm=1024, tn=1024,
                                        name=f"in_proj_{layer}")
            lam_init = 0.8 - 0.6 * math.exp(-0.3 * layer)
            casts = cast_later if w_up_all is None else ()
            att = _diff_attention(qkv.reshape(bsz, slen, -1), rel_bias, diff_lq1[i], diff_lk1[i],
                                  diff_lq2[i], diff_lk2[i], diff_head_norm[i],
                                  lam_init=lam_init, t=512, name=f"diff_attn_{layer}", casts=casts)
            if casts:
                att, w_out_all, w_pw1_all, w_pw2_all, w_up_all, w_down_all = att
            ssm = _s5_mixer(u, s5_lambda_re[i], s5_lambda_im[i], s5_log_step[i], s5_b_re[i],
                            s5_b_im[i], s5_c_re[i], s5_c_im[i], s5_d[i], s5_w_glu[i], s5_b_glu[i],
                            bsz=bsz, slen=slen, name=f"s5_{layer}")
            xf, h_ffn = _matmul_residual_norm(
                [(att.reshape(m, att_width), 0), (ssm, 1)], w_out_all, i, xf, norm_ffn[layer],
                tm=512, name=f"out_proj_{layer}")
        else:
            z = _norm_glu_matmul(xf, g_mix, w_pw1_all, i, tm=1024, tn=1024, name=f"conv_pw1_{layer}")
            c = _conv_ln_silu(z.reshape(bsz, slen, d_model), conv_w_dw[i], conv_b_dw[i],
                              conv_ln_g[i], conv_ln_b[i], ts=256, name=f"conv_dw_{layer}")
            xf, h_ffn = _matmul_residual_norm(
                [(c.reshape(m, d_model), 0)], w_pw2_all, i, xf, norm_ffn[layer],
                tm=512, name=f"conv_pw2_{layer}")
        xf = _conv_ffn(xf, h_ffn, w_up_all, ffn_w_dw, ffn_b_dw, w_down_all, layer,
                       bsz=bsz, slen=slen, name=f"ffn_{layer}")
    out = _rmsnorm(xf, norm_final, tm=512, name="final_norm")
    return out.reshape(bsz, slen, d_model).astype(x.dtype)
```
